```python
import jax, jax.numpy as jnp
from jax import lax
import numpy as np

D_MODEL = 1024
BATCH = 8
SEQ = 2048
DEPTH = 4
DEC_BATCH = 128
DEC_SEQ = 1
PAST_LEN = 16384
PAGE_SIZE = 128

W_A = D_MODEL // 4
W_B = D_MODEL // 4
W_C = D_MODEL // 2
MIX_WIDTH = W_A + W_B + W_C
K_A = 3
K_B = 31
K_C = 4
LRU_HEADS = 8
LRU_HD = W_C // LRU_HEADS
LRU_C = 8.0
D_FF = 2816
RMS_EPS = 1e-6
LN_EPS = 1e-5
IN_SIZES = (W_A, W_A, W_A, W_B, W_B, W_C, W_C)
IN_WIDTH = sum(IN_SIZES)
IN_SPLITS = [int(s) for s in np.cumsum(IN_SIZES)[:-1]]

kernel_name = "hybrid_conv_lru_macaron_decode_step"


def rms_norm(x, g):
    xf = x.astype(jnp.float32)
    y = xf * lax.rsqrt(jnp.mean(xf * xf, axis=-1, keepdims=True) + RMS_EPS)
    return (y * g.astype(jnp.float32)).astype(x.dtype)


def layer_norm(x, g, b):
    xf = x.astype(jnp.float32)
    mu = jnp.mean(xf, axis=-1, keepdims=True)
    xc = xf - mu
    var = jnp.mean(xc * xc, axis=-1, keepdims=True)
    y = xc * lax.rsqrt(var + LN_EPS) * g.astype(jnp.float32) + b.astype(jnp.float32)
    return y.astype(x.dtype)


def swiglu(h, w_up, w_down):
    gu = h @ w_up
    g, u = jnp.split(gu, 2, axis=-1)
    return (jax.nn.silu(g) * u) @ w_down


def causal_dwconv(x, buf, w):
    K, C = w.shape
    xp = jnp.concatenate([buf.astype(x.dtype), x], axis=1)
    y = lax.conv_general_dilated(xp, w[:, None, :].astype(x.dtype), window_strides=(1,), padding='VALID',
                                 dimension_numbers=('NWC', 'WIO', 'NWC'), feature_group_count=C)
    return y, xp[:, xp.shape[1] - (K - 1):]


def lru_scan(a, b, h0):
    def comb(l, r):
        return (l[0] * r[0], r[0] * l[1] + r[1])
    A, Bc = lax.associative_scan(comb, (a, b), axis=1)
    return A * h0[:, None, :] + Bc


def block_diag(x, w):
    Bsz, T, C = x.shape
    y = jnp.einsum('bthi,hij->bthj', x.reshape(Bsz, T, LRU_HEADS, LRU_HD), w)
    return y.reshape(Bsz, T, C)


def mixer(h, buf_a, buf_b, buf_c, h0, w_in, conv_a_w, conv_b_w, conv_b_b, ln_b_g, ln_b_b,
          conv_c_w, conv_c_b, lru_wa, lru_ba, lru_wx, lru_bx, lru_lam, grp_g, w_out):
    u = h @ w_in
    a_b, a_c, a_x, b_v, b_g, c_g, c_x = jnp.split(u, IN_SPLITS, axis=-1)
    conv_a, nbuf_a = causal_dwconv(a_c * a_x, buf_a, conv_a_w)
    y_a = a_b * conv_a
    glu = b_v * jax.nn.sigmoid(b_g)
    conv_b, nbuf_b = causal_dwconv(glu, buf_b, conv_b_w)
    y_b = jax.nn.silu(layer_norm(conv_b + conv_b_b, ln_b_g, ln_b_b))
    xc, nbuf_c = causal_dwconv(c_x, buf_c, conv_c_w)
    xc = xc + conv_c_b
    r = jax.nn.sigmoid((block_diag(xc, lru_wa) + lru_ba).astype(jnp.float32))
    i = jax.nn.sigmoid((block_diag(xc, lru_wx) + lru_bx).astype(jnp.float32))
    log_a = -LRU_C * r * jax.nn.softplus(-lru_lam.astype(jnp.float32))
    a = jnp.exp(log_a)
    b = jnp.sqrt(jnp.maximum(-jnp.expm1(2.0 * log_a), 0.0)) * (i * xc.astype(jnp.float32))
    hs = lru_scan(a, b, h0.astype(jnp.float32))
    y_c = jax.nn.gelu(c_g) * hs.astype(h.dtype)
    y = jnp.concatenate([rms_norm(y_a, grp_g[:W_A]),
                         rms_norm(y_b, grp_g[W_A:W_A + W_B]),
                         rms_norm(y_c, grp_g[W_A + W_B:])], axis=-1)
    return y @ w_out, nbuf_a, nbuf_b, nbuf_c, hs[:, -1].astype(h0.dtype)


def run_trunk(x, st_a, st_b, st_c, st_h, norm_ffn1, w1_up, w1_down, norm_mix, w_in, conv_a_w, conv_b_w,
              conv_b_b, ln_b_g, ln_b_b, conv_c_w, conv_c_b, lru_wa, lru_ba, lru_wx, lru_bx, lru_lam, grp_g,
              w_out, norm_ffn2, w2_up, w2_down, final_norm):
    na, nb, nc, nh = [], [], [], []
    for l in range(DEPTH):
        x = x + 0.5 * swiglu(rms_norm(x, norm_ffn1[l]), w1_up[l], w1_down[l])
        m, ba, bb, bc, hl = mixer(rms_norm(x, norm_mix[l]), st_a[l], st_b[l], st_c[l], st_h[l], w_in[l],
                                  conv_a_w[l], conv_b_w[l], conv_b_b[l], ln_b_g[l], ln_b_b[l], conv_c_w[l],
                                  conv_c_b[l], lru_wa[l], lru_ba[l], lru_wx[l], lru_bx[l], lru_lam[l],
                                  grp_g[l], w_out[l])
        x = x + m
        x = x + 0.5 * swiglu(rms_norm(x, norm_ffn2[l]), w2_up[l], w2_down[l])
        na.append(ba); nb.append(bb); nc.append(bc); nh.append(hl)
    return (rms_norm(x, final_norm), jnp.stack(na), jnp.stack(nb), jnp.stack(nc), jnp.stack(nh))


def setup_inputs(seed: int = 0) -> dict:
    key = jax.random.key(seed)
    ks = jax.random.split(key, 40)
    f32 = jnp.float32

    def nrm(k, shape, scale):
        return jax.random.normal(k, shape, f32) * scale

    def gain(k, shape):
        return 1.0 + 0.01 * jax.random.normal(k, shape, f32)

    u = jax.random.uniform(ks[39], (DEPTH, W_C), f32, minval=0.9, maxval=0.999)
    a0 = u ** (1.0 / LRU_C)
    lam = jnp.log(a0) - jnp.log1p(-a0)
    return {
        "x_prompt": nrm(ks[0], (BATCH, SEQ, D_MODEL), 1.0),
        "x_sample": nrm(ks[1], (DEC_BATCH, DEC_SEQ, D_MODEL), 1.0),
        "state_conv_a": nrm(ks[2], (DEPTH, DEC_BATCH, K_A - 1, W_A), 1.0),
        "state_conv_b": nrm(ks[3], (DEPTH, DEC_BATCH, K_B - 1, W_B), 1.0),
        "state_conv_c": nrm(ks[4], (DEPTH, DEC_BATCH, K_C - 1, W_C), 1.0),
        "state_lru_h": nrm(ks[5], (DEPTH, DEC_BATCH, W_C), 0.5),
        "norm_ffn1": gain(ks[6], (DEPTH, D_MODEL)),
        "w1_up": nrm(ks[7], (DEPTH, D_MODEL, 2 * D_FF), D_MODEL ** -0.5),
        "w1_down": nrm(ks[8], (DEPTH, D_FF, D_MODEL), D_FF ** -0.5),
        "norm_mix": gain(ks[9], (DEPTH, D_MODEL)),
        "w_in": nrm(ks[10], (DEPTH, D_MODEL, IN_WIDTH), D_MODEL ** -0.5),
        "conv_a_w": nrm(ks[11], (DEPTH, K_A, W_A), K_A ** -0.5),
        "conv_b_w": nrm(ks[12], (DEPTH, K_B, W_B), K_B ** -0.5),
        "conv_b_b": nrm(ks[13], (DEPTH, W_B), 0.01),
        "ln_b_g": gain(ks[14], (DEPTH, W_B)),
        "ln_b_b": nrm(ks[15], (DEPTH, W_B), 0.01),
        "conv_c_w": nrm(ks[16], (DEPTH, K_C, W_C), K_C ** -0.5),
        "conv_c_b": nrm(ks[17], (DEPTH, W_C), 0.01),
        "lru_wa": nrm(ks[18], (DEPTH, LRU_HEADS, LRU_HD, LRU_HD), LRU_HD ** -0.5),
        "lru_ba": nrm(ks[19], (DEPTH, W_C), 0.01),
        "lru_wx": nrm(ks[20], (DEPTH, LRU_HEADS, LRU_HD, LRU_HD), LRU_HD ** -0.5),
        "lru_bx": nrm(ks[21], (DEPTH, W_C), 0.01),
        "lru_lam": lam,
        "grp_g": gain(ks[22], (DEPTH, MIX_WIDTH)),
        "w_out": nrm(ks[23], (DEPTH, MIX_WIDTH, D_MODEL), MIX_WIDTH ** -0.5),
        "norm_ffn2": gain(ks[24], (DEPTH, D_MODEL)),
        "w2_up": nrm(ks[25], (DEPTH, D_MODEL, 2 * D_FF), D_MODEL ** -0.5),
        "w2_down": nrm(ks[26], (DEPTH, D_FF, D_MODEL), D_FF ** -0.5),
        "final_norm": gain(ks[27], (D_MODEL,)),
    }


def reference(x_prompt, x_sample, state_conv_a, state_conv_b, state_conv_c, state_lru_h,
              norm_ffn1, w1_up, w1_down, norm_mix, w_in, conv_a_w, conv_b_w, conv_b_b, ln_b_g, ln_b_b,
              conv_c_w, conv_c_b, lru_wa, lru_ba, lru_wx, lru_bx, lru_lam, grp_g, w_out,
              norm_ffn2, w2_up, w2_down, final_norm):
    weights = (norm_ffn1, w1_up, w1_down, norm_mix, w_in, conv_a_w, conv_b_w, conv_b_b, ln_b_g, ln_b_b,
               conv_c_w, conv_c_b, lru_wa, lru_ba, lru_wx, lru_bx, lru_lam, grp_g, w_out,
               norm_ffn2, w2_up, w2_down, final_norm)
    bp = x_prompt.shape[0]
    zp_a = jnp.zeros((DEPTH, bp, K_A - 1, W_A), x_prompt.dtype)
    zp_b = jnp.zeros((DEPTH, bp, K_B - 1, W_B), x_prompt.dtype)
    zp_c = jnp.zeros((DEPTH, bp, K_C - 1, W_C), x_prompt.dtype)
    zp_h = jnp.zeros((DEPTH, bp, W_C), x_prompt.dtype)
    y_prompt, p_conv_a, p_conv_b, p_conv_c, p_lru_h = run_trunk(x_prompt, zp_a, zp_b, zp_c, zp_h, *weights)
    y_sample, s_conv_a, s_conv_b, s_conv_c, s_lru_h = run_trunk(x_sample, state_conv_a, state_conv_b,
                                                               state_conv_c, state_lru_h, *weights)
    return (y_prompt, y_sample, p_conv_a, p_conv_b, p_conv_c, p_lru_h, s_conv_a, s_conv_b, s_conv_c, s_lru_h)
```

```python
import functools

import jax
import jax.numpy as jnp
from jax import lax
from jax.experimental import pallas as pl
from jax.experimental.pallas import tpu as pltpu

D_MODEL = 1024
DEPTH = 4
W_A = 256
W_B = 256
W_C = 512
K_A = 3
K_B = 31
K_C = 4
LRU_HEADS = 8
LRU_HD = W_C // LRU_HEADS
LRU_C = 8.0
D_FF = 2816
RMS_EPS = 1e-6
LN_EPS = 1e-5
IN_WIDTH = 3 * W_A + 2 * W_B + 2 * W_C

FF_CHUNK = 256
N_FF_CHUNKS = D_FF // FF_CHUNK
SUBLANES = 8
VMEM_LIMIT_BYTES = 56 * 1024 * 1024

F32 = jnp.float32
BF16 = jnp.bfloat16


def _rms(x, g):
    return x * lax.rsqrt(jnp.mean(x * x, axis=-1, keepdims=True) + RMS_EPS) * g


def _sigmoid(x):
    return 1.0 / (1.0 + jnp.exp(-x))


def _const_spec(shape, layer):
    nd = len(shape)
    return pl.BlockSpec((None,) + tuple(shape), lambda i: (layer,) + (0,) * nd,
                        pipeline_mode=pl.Buffered(1))


def _ffn_kernel(x_ref, g_ref, wg_ref, wu_ref, wd_ref, gf_ref, o_ref, *, final):
    x = x_ref[...]
    h = _rms(x, g_ref[...]).astype(BF16)
    acc = jnp.zeros(x.shape, F32)
    for c in range(N_FF_CHUNKS):
        g = jnp.dot(h, wg_ref[c], preferred_element_type=F32)
        u = jnp.dot(h, wu_ref[c], preferred_element_type=F32)
        a = (g * _sigmoid(g) * u).astype(BF16)
        acc = acc + jnp.dot(a, wd_ref[c], preferred_element_type=F32)
    y = x + 0.5 * acc
    if final:
        y = _rms(y, gf_ref[...])
    o_ref[...] = y


def _ffn_call(x, layer, norm_g, wg, wu, wd, final_g, *, tm, final):
    m = x.shape[0]
    return pl.pallas_call(
        functools.partial(_ffn_kernel, final=final),
        grid=(m // tm,),
        in_specs=[
            pl.BlockSpec((tm, D_MODEL), lambda i: (i, 0)),
            _const_spec((1, D_MODEL), layer),
            _const_spec((N_FF_CHUNKS, D_MODEL, FF_CHUNK), layer),
            _const_spec((N_FF_CHUNKS, D_MODEL, FF_CHUNK), layer),
            _const_spec((N_FF_CHUNKS, FF_CHUNK, D_MODEL), layer),
            pl.BlockSpec((1, D_MODEL), lambda i: (0, 0)),
        ],
        out_specs=pl.BlockSpec((tm, D_MODEL), lambda i: (i, 0)),
        out_shape=jax.ShapeDtypeStruct((m, D_MODEL), F32),
        compiler_params=pltpu.CompilerParams(
            dimension_semantics=("arbitrary",), vmem_limit_bytes=VMEM_LIMIT_BYTES),
        name="ffn",
    )(x, norm_g, wg, wu, wd, final_g)


def _mixer_kernel(x_ref, ia_ref, ib_ref, ic_ref, ih_ref,
                  gmix_ref, win_ref, caw_ref, cbw_ref, cbb_ref, lng_ref, lnb_ref,
                  ccw_ref, ccb_ref, wa_ref, ba_ref, wx_ref, bx_ref, lam_ref, gg_ref, wout_ref,
                  o_ref, oa_ref, ob_ref, oc_ref, oh_ref,
                  u_ref, pa_ref, gb_ref, cx_ref, a_ref, b_ref, h_ref, y_ref,
                  *, nb, tm, rb):
    i = pl.program_id(0)
    ha, hb, hc = (K_A - 1) * nb, (K_B - 1) * nb, (K_C - 1) * nb
    n_rb = tm // rb
    tt = tm // nb

    @pl.when(i == 0)
    def _():
        pa_ref[0:ha, :] = ia_ref[...]
        gb_ref[0:hb, :] = ib_ref[...]
        cx_ref[0:hc, :] = ic_ref[...]
        h_ref[...] = ih_ref[...]

    x = x_ref[...]
    hx = _rms(x, gmix_ref[...]).astype(BF16)
    u_ref[...] = jnp.dot(hx, win_ref[...], preferred_element_type=F32)

    def stage(r, carry):
        r0 = pl.multiple_of(r * rb, rb)
        rows = pl.ds(r0, rb)
        pa_ref[pl.ds(ha + r0, rb), :] = u_ref[rows, 256:512] * u_ref[rows, 512:768]
        gb_ref[pl.ds(hb + r0, rb), :] = u_ref[rows, 768:1024] * _sigmoid(u_ref[rows, 1024:1280])
        cx_ref[pl.ds(hc + r0, rb), :] = u_ref[rows, 1792:2304]
        return carry
    lax.fori_loop(0, n_rb, stage, 0)

    sp = lam_ref[...]
    sp = jnp.maximum(-sp, 0.0) + jnp.log1p(jnp.exp(-jnp.abs(sp)))

    def groups(r, carry):
        r0 = pl.multiple_of(r * rb, rb)
        rows = pl.ds(r0, rb)
        ca = pa_ref[pl.ds(r0, rb), :] * caw_ref[0:1, :]
        for k in range(1, K_A):
            ca = ca + pa_ref[pl.ds(r0 + k * nb, rb), :] * caw_ref[k:k + 1, :]
        ya = u_ref[rows, 0:256] * ca
        y_ref[rows, 0:256] = _rms(ya, gg_ref[:, 0:256]).astype(BF16)
        cb = gb_ref[pl.ds(r0, rb), :] * cbw_ref[0:1, :]
        for k in range(1, K_B):
            cb = cb + gb_ref[pl.ds(r0 + k * nb, rb), :] * cbw_ref[k:k + 1, :]
        cb = cb + cbb_ref[...]
        mu = jnp.mean(cb, axis=-1, keepdims=True)
        cc = cb - mu
        var = jnp.mean(cc * cc, axis=-1, keepdims=True)
        ln = cc * lax.rsqrt(var + LN_EPS) * lng_ref[...] + lnb_ref[...]
        yb = ln * _sigmoid(ln)
        y_ref[rows, 256:512] = _rms(yb, gg_ref[:, 256:512]).astype(BF16)
        xc = cx_ref[pl.ds(r0, rb), :] * ccw_ref[0:1, :]
        for k in range(1, K_C):
            xc = xc + cx_ref[pl.ds(r0 + k * nb, rb), :] * ccw_ref[k:k + 1, :]
        xc = xc + ccb_ref[...]
        xcb = xc.astype(BF16)
        rg = _sigmoid(jnp.dot(xcb, wa_ref[...], preferred_element_type=F32) + ba_ref[...])
        ig = _sigmoid(jnp.dot(xcb, wx_ref[...], preferred_element_type=F32) + bx_ref[...])
        log_a = (-LRU_C) * rg * sp
        a = jnp.exp(log_a)
        gain = jnp.sqrt(jnp.maximum(jnp.tanh(-log_a) * (1.0 + a * a), 0.0))
        a_ref[rows, :] = a
        b_ref[rows, :] = gain * (ig * xc)
        return carry
    lax.fori_loop(0, n_rb, groups, 0)

    def scan(t, h):
        rows = pl.ds(pl.multiple_of(t * nb, nb), nb)
        h = a_ref[rows, :] * h + b_ref[rows, :]
        b_ref[rows, :] = h
        return h
    h_ref[...] = lax.fori_loop(0, tt, scan, h_ref[...], unroll=min(tt, 8))

    def gate(r, carry):
        r0 = pl.multiple_of(r * rb, rb)
        rows = pl.ds(r0, rb)
        cg = u_ref[rows, 1280:1792]
        gelu = 0.5 * cg * (1.0 + jnp.tanh(0.7978845608028654 * (cg + 0.044715 * (cg * cg * cg))))
        yc = gelu * b_ref[rows, :]
        y_ref[rows, 512:1024] = _rms(yc, gg_ref[:, 512:1024]).astype(BF16)
        return carry
    lax.fori_loop(0, n_rb, gate, 0)

    o_ref[...] = x + jnp.dot(y_ref[...], wout_ref[...], preferred_element_type=F32)

    ta = pa_ref[tm:tm + ha, :]
    tb = gb_ref[tm:tm + hb, :]
    tc = cx_ref[tm:tm + hc, :]
    pa_ref[0:ha, :] = ta
    gb_ref[0:hb, :] = tb
    cx_ref[0:hc, :] = tc

    @pl.when(i == pl.num_programs(0) - 1)
    def _():
        oa_ref[...] = ta
        ob_ref[...] = tb
        oc_ref[...] = tc
        oh_ref[...] = h_ref[...]


def _mixer_call(x, init_a, init_b, init_c, init_h, layer, w, *, nb, tm):
    m = x.shape[0]
    ha, hb, hc = (K_A - 1) * nb, (K_B - 1) * nb, (K_C - 1) * nb
    rb = max(nb, 64)
    full = lambda shape: pl.BlockSpec(shape, lambda i: (0,) * len(shape))
    return pl.pallas_call(
        functools.partial(_mixer_kernel, nb=nb, tm=tm, rb=rb),
        grid=(m // tm,),
        in_specs=[
            pl.BlockSpec((tm, D_MODEL), lambda i: (i, 0)),
            full((ha, W_A)), full((hb, W_B)), full((hc, W_C)), full((nb, W_C)),
            _const_spec((1, D_MODEL), layer),
            _const_spec((D_MODEL, IN_WIDTH), layer),
            _const_spec((K_A, W_A), layer),
            _const_spec((K_B, W_B), layer),
            _const_spec((1, W_B), layer),
            _const_spec((1, W_B), layer),
            _const_spec((1, W_B), layer),
            _const_spec((K_C, W_C), layer),
            _const_spec((1, W_C), layer),
            _const_spec((W_C, W_C), layer),
            _const_spec((1, W_C), layer),
            _const_spec((W_C, W_C), layer),
            _const_spec((1, W_C), layer),
            _const_spec((1, W_C), layer),
            _const_spec((1, D_MODEL), layer),
            _const_spec((D_MODEL, D_MODEL), layer),
        ],
        out_specs=[
            pl.BlockSpec((tm, D_MODEL), lambda i: (i, 0)),
            full((ha, W_A)), full((hb, W_B)), full((hc, W_C)), full((nb, W_C)),
        ],
        out_shape=[
            jax.ShapeDtypeStruct((m, D_MODEL), F32),
            jax.ShapeDtypeStruct((ha, W_A), F32),
            jax.ShapeDtypeStruct((hb, W_B), F32),
            jax.ShapeDtypeStruct((hc, W_C), F32),
            jax.ShapeDtypeStruct((nb, W_C), F32),
        ],
        scratch_shapes=[
            pltpu.VMEM((tm, IN_WIDTH), F32),
            pltpu.VMEM((ha + tm, W_A), F32),
            pltpu.VMEM((hb + tm, W_B), F32),
            pltpu.VMEM((hc + tm, W_C), F32),
            pltpu.VMEM((tm, W_C), F32),
            pltpu.VMEM((tm, W_C), F32),
            pltpu.VMEM((nb, W_C), F32),
            pltpu.VMEM((tm, D_MODEL), BF16),
        ],
        compiler_params=pltpu.CompilerParams(
            dimension_semantics=("arbitrary",), vmem_limit_bytes=VMEM_LIMIT_BYTES),
        name="mixer",
    )(x, init_a, init_b, init_c, init_h, *w)


def _block_diag(w):
    eye = jnp.eye(LRU_HEADS, dtype=w.dtype)
    full = jnp.einsum('lhij,hg->lhigj', w, eye)
    return full.reshape(DEPTH, W_C, W_C).astype(BF16)


def _run_trunk(x, st_a, st_b, st_c, st_h, p, *, nb, tm_ffn, tm_mix):
    outs_a, outs_b, outs_c, outs_h = [], [], [], []
    for l in range(DEPTH):
        x = _ffn_call(x, l, p['g1'], p['wg1'], p['wu1'], p['wd1'], p['gf'], tm=tm_ffn, final=False)
        x, oa, ob, oc, oh = _mixer_call(
            x, st_a[l].reshape(-1, W_A), st_b[l].reshape(-1, W_B), st_c[l].reshape(-1, W_C), st_h[l],
            l, p['mix'], nb=nb, tm=tm_mix)
        x = _ffn_call(x, l, p['g2'], p['wg2'], p['wu2'], p['wd2'], p['gf'], tm=tm_ffn,
                      final=(l == DEPTH - 1))
        outs_a.append(oa.reshape(K_A - 1, nb, W_A))
        outs_b.append(ob.reshape(K_B - 1, nb, W_B))
        outs_c.append(oc.reshape(K_C - 1, nb, W_C))
        outs_h.append(oh)
    to_seq_major = lambda s: jnp.stack(s).transpose(0, 2, 1, 3)
    return x, to_seq_major(outs_a), to_seq_major(outs_b), to_seq_major(outs_c), jnp.stack(outs_h)


def kernel(x_prompt, x_sample, state_conv_a, state_conv_b, state_conv_c, state_lru_h, norm_ffn1, w1_up, w1_down, norm_mix, w_in, conv_a_w, conv_b_w, conv_b_b, ln_b_g, ln_b_b, conv_c_w, conv_c_b, lru_wa, lru_ba, lru_wx, lru_bx, lru_lam, grp_g, w_out, norm_ffn2, w2_up, w2_down, final_norm):
    bp, seq, _ = x_prompt.shape
    bs = x_sample.shape[0]

    def up_chunks(w, lo):
        w = w[:, :, lo:lo + D_FF].reshape(DEPTH, D_MODEL, N_FF_CHUNKS, FF_CHUNK)
        return w.transpose(0, 2, 1, 3).astype(BF16)

    def down_chunks(w):
        return w.reshape(DEPTH, N_FF_CHUNKS, FF_CHUNK, D_MODEL).astype(BF16)

    row = lambda v: v[:, None, :]
    p = {
        'g1': row(norm_ffn1), 'wg1': up_chunks(w1_up, 0), 'wu1': up_chunks(w1_up, D_FF), 'wd1': down_chunks(w1_down),
        'g2': row(norm_ffn2), 'wg2': up_chunks(w2_up, 0), 'wu2': up_chunks(w2_up, D_FF), 'wd2': down_chunks(w2_down),
        'gf': final_norm[None, :],
        'mix': (row(norm_mix), w_in.astype(BF16), conv_a_w, conv_b_w, row(conv_b_b), row(ln_b_g), row(ln_b_b),
                conv_c_w, row(conv_c_b), _block_diag(lru_wa), row(lru_ba), _block_diag(lru_wx), row(lru_bx),
                row(lru_lam), row(grp_g), w_out.astype(BF16)),
    }

    xp = x_prompt.transpose(1, 0, 2).reshape(seq * bp, D_MODEL)
    zeros = lambda k, c: jnp.zeros((DEPTH, k, bp, c), F32)
    yp, pa, pb, pc, ph = _run_trunk(xp, zeros(K_A - 1, W_A), zeros(K_B - 1, W_B), zeros(K_C - 1, W_C),
                                    jnp.zeros((DEPTH, bp, W_C), F32), p, nb=bp, tm_ffn=512, tm_mix=512)
    y_prompt = yp.reshape(seq, bp, D_MODEL).transpose(1, 0, 2)

    xs = x_sample.reshape(bs, D_MODEL)
    tmaj = lambda s: s.transpose(0, 2, 1, 3)
    ys, sa, sb, sc, sh = _run_trunk(xs, tmaj(state_conv_a), tmaj(state_conv_b), tmaj(state_conv_c),
                                    state_lru_h, p, nb=bs, tm_ffn=bs, tm_mix=bs)
    y_sample = ys.reshape(bs, 1, D_MODEL)
    return (y_prompt, y_sample, pa, pb, pc, ph, sa, sb, sc, sh)
```

```python
import functools

import jax
import jax.numpy as jnp
from jax import lax
from jax.experimental import pallas as pl
from jax.experimental.pallas import tpu as pltpu

D_MODEL = 1024
DEPTH = 4
W_A = 256
W_B = 256
W_C = 512
K_A = 3
K_B = 31
K_C = 4
LRU_HEADS = 8
LRU_HD = W_C // LRU_HEADS
LRU_C = 8.0
D_FF = 2816
RMS_EPS = 1e-6
LN_EPS = 1e-5
IN_WIDTH = 3 * W_A + 2 * W_B + 2 * W_C

FF_CHUNK = 256
N_FF_CHUNKS = D_FF // FF_CHUNK
MM_CHUNK = 256
SUBLANES = 8
VMEM_LIMIT_BYTES = 56 * 1024 * 1024

F32 = jnp.float32
BF16 = jnp.bfloat16


def _rms(x, g):
    return x * lax.rsqrt(jnp.mean(x * x, axis=-1, keepdims=True) + RMS_EPS) * g


def _sigmoid(x):
    return 1.0 / (1.0 + jnp.exp(-x))


def _const_spec(shape, layer):
    nd = len(shape)
    return pl.BlockSpec((None,) + tuple(shape), lambda i: (layer,) + (0,) * nd,
                        pipeline_mode=pl.Buffered(1))


def _ffn_kernel(x_ref, g_ref, wup_ref, wd_ref, gf_ref, o_ref, *, final):
    x = x_ref[...]
    h = _rms(x, g_ref[...]).astype(BF16)
    acc = jnp.zeros(x.shape, F32)
    for c in range(N_FF_CHUNKS):
        lo = c * FF_CHUNK
        g = jnp.dot(h, wup_ref[:, lo:lo + FF_CHUNK], preferred_element_type=F32)
        u = jnp.dot(h, wup_ref[:, D_FF + lo:D_FF + lo + FF_CHUNK], preferred_element_type=F32)
        a = (g * _sigmoid(g) * u).astype(BF16)
        acc = acc + jnp.dot(a, wd_ref[lo:lo + FF_CHUNK, :], preferred_element_type=F32)
    y = x + 0.5 * acc
    if final:
        y = _rms(y, gf_ref[...])
    o_ref[...] = y


def _ffn_call(x, layer, norm_g, wup, wd, final_g, *, tm, final):
    m = x.shape[0]
    rows_spec = pl.BlockSpec((tm, D_MODEL), lambda i: (i, 0))
    return pl.pallas_call(
        functools.partial(_ffn_kernel, final=final),
        grid=(m // tm,),
        in_specs=[
            rows_spec,
            _const_spec((1, D_MODEL), layer),
            _const_spec((D_MODEL, 2 * D_FF), layer),
            _const_spec((D_FF, D_MODEL), layer),
            pl.BlockSpec((1, D_MODEL), lambda i: (0, 0)),
        ],
        out_specs=rows_spec,
        out_shape=jax.ShapeDtypeStruct((m, D_MODEL), F32),
        compiler_params=pltpu.CompilerParams(
            dimension_semantics=("arbitrary",), vmem_limit_bytes=VMEM_LIMIT_BYTES),
        name="ffn",
    )(x, norm_g, wup, wd, final_g)


def _mixer_kernel(x_ref, xr_ref, ia_ref, ib_ref, ic_ref, ih_ref,
                  gmix_ref, win_ref, caw_ref, cbw_ref, cbb_ref, lng_ref, lnb_ref,
                  ccw_ref, ccb_ref, wa_ref, ba_ref, wx_ref, bx_ref, lam_ref, gg_ref, wout_ref,
                  o_ref, oa_ref, ob_ref, oc_ref, oh_ref,
                  u0_ref, u1_ref, pa_ref, gb_ref, cx_ref, xc_ref, xcb_ref, g_ref, a_ref, b_ref, h_ref,
                  y0_ref, y1_ref, hx_ref, *, nb, tm, rb, gm, n_tiles):
    i = pl.program_id(0)
    ha, hb, hc = (K_A - 1) * nb, (K_B - 1) * nb, (K_C - 1) * nb

    @pl.when(i == 0)
    def _():
        pa_ref[0:ha, :] = ia_ref[...]
        gb_ref[0:hb, :] = ib_ref[...]
        cx_ref[0:hc, :] = ic_ref[...]
        h_ref[...] = ih_ref[...]
        u1_ref[...] = jnp.zeros((tm, IN_WIDTH), F32)
        y0_ref[...] = jnp.zeros((tm, D_MODEL), BF16)
        y1_ref[...] = jnp.zeros((tm, D_MODEL), BF16)

    step = functools.partial(
        _mixer_step, x_ref, xr_ref, gmix_ref, win_ref, caw_ref, cbw_ref, cbb_ref, lng_ref, lnb_ref,
        ccw_ref, ccb_ref, wa_ref, ba_ref, wx_ref, bx_ref, lam_ref, gg_ref, wout_ref, o_ref,
        pa_ref, gb_ref, cx_ref, xc_ref, xcb_ref, g_ref, a_ref, b_ref, h_ref, hx_ref,
        nb=nb, tm=tm, rb=rb, gm=gm, n_tiles=n_tiles)

    @pl.when(i % 2 == 0)
    def _():
        step(u0_ref, u1_ref, y0_ref, y1_ref)

    @pl.when(i % 2 == 1)
    def _():
        step(u1_ref, u0_ref, y1_ref, y0_ref)

    @pl.when(i == pl.num_programs(0) - 1)
    def _():
        oa_ref[...] = pa_ref[0:ha, :]
        ob_ref[...] = gb_ref[0:hb, :]
        oc_ref[...] = cx_ref[0:hc, :]
        oh_ref[...] = h_ref[...]


def _mixer_step(x_ref, xr_ref, gmix_ref, win_ref, caw_ref, cbw_ref, cbb_ref, lng_ref, lnb_ref,
                ccw_ref, ccb_ref, wa_ref, ba_ref, wx_ref, bx_ref, lam_ref, gg_ref, wout_ref, o_ref,
                pa_ref, gb_ref, cx_ref, xc_ref, xcb_ref, g_ref, a_ref, b_ref, h_ref, hx_ref,
                u_new, us, y_old, ys, *, nb, tm, rb, gm, n_tiles):
    i = pl.program_id(0)
    ha, hb, hc = (K_A - 1) * nb, (K_B - 1) * nb, (K_C - 1) * nb
    n_rb = tm // rb
    tt = tm // nb

    hx_ref[...] = _rms(x_ref[...], gmix_ref[...]).astype(BF16)

    def in_proj(c):
        cols = slice(c * MM_CHUNK, (c + 1) * MM_CHUNK)
        u_new[:, cols] = jnp.dot(hx_ref[...], win_ref[:, cols], preferred_element_type=F32)

    def out_proj(c):
        cols = slice(c * MM_CHUNK, (c + 1) * MM_CHUNK)
        o_ref[:, cols] = xr_ref[:, cols] + jnp.dot(y_old[...], wout_ref[:, cols],
                                                   preferred_element_type=F32)

    sp = lam_ref[...]
    sp = jnp.maximum(-sp, 0.0) + jnp.log1p(jnp.exp(-jnp.abs(sp)))

    def convs(r):
        r0 = r * rb
        rows = slice(r0, r0 + rb)
        pa_ref[ha + r0:ha + r0 + rb, :] = us[rows, 256:512] * us[rows, 512:768]
        gb_ref[hb + r0:hb + r0 + rb, :] = us[rows, 768:1024] * _sigmoid(us[rows, 1024:1280])
        cx_ref[hc + r0:hc + r0 + rb, :] = us[rows, 1792:2304]
        ca = pa_ref[r0:r0 + rb, :] * caw_ref[0:1, :]
        for k in range(1, K_A):
            ca = ca + pa_ref[r0 + k * nb:r0 + k * nb + rb, :] * caw_ref[k:k + 1, :]
        ya = us[rows, 0:256] * ca
        ys[rows, 0:256] = _rms(ya, gg_ref[:, 0:256]).astype(BF16)
        cb = gb_ref[r0:r0 + rb, :] * cbw_ref[0:1, :]
        for k in range(1, K_B):
            cb = cb + gb_ref[r0 + k * nb:r0 + k * nb + rb, :] * cbw_ref[k:k + 1, :]
        cb = cb + cbb_ref[...]
        mu = jnp.mean(cb, axis=-1, keepdims=True)
        cc = cb - mu
        var = jnp.mean(cc * cc, axis=-1, keepdims=True)
        ln = cc * lax.rsqrt(var + LN_EPS) * lng_ref[...] + lnb_ref[...]
        yb = ln * _sigmoid(ln)
        ys[rows, 256:512] = _rms(yb, gg_ref[:, 256:512]).astype(BF16)
        xc = cx_ref[r0:r0 + rb, :] * ccw_ref[0:1, :]
        for k in range(1, K_C):
            xc = xc + cx_ref[r0 + k * nb:r0 + k * nb + rb, :] * ccw_ref[k:k + 1, :]
        xc = xc + ccb_ref[...]
        xc_ref[rows, :] = xc
        xcb_ref[rows, :] = xc.astype(BF16)
        if (r0 + rb) % gm == 0:
            grows = slice(r0 + rb - gm, r0 + rb)
            for half in range(2):
                cols = slice(half * 256, half * 256 + 256)
                g_ref[grows, cols] = jnp.dot(xcb_ref[grows, cols], wa_ref[half],
                                             preferred_element_type=F32)
                g_ref[grows, 512 + half * 256:768 + half * 256] = jnp.dot(
                    xcb_ref[grows, cols], wx_ref[half], preferred_element_type=F32)

    def coeffs(r):
        rows = slice(r * rb, r * rb + rb)
        rg = _sigmoid(g_ref[rows, 0:512] + ba_ref[...])
        ig = _sigmoid(g_ref[rows, 512:1024] + bx_ref[...])
        log_a = (-LRU_C) * rg * sp
        a = jnp.exp(log_a)
        gain = jnp.sqrt(jnp.maximum(jnp.tanh(-log_a) * (1.0 + a * a), 0.0))
        a_ref[rows, :] = a
        b_ref[rows, :] = gain * (ig * xc_ref[rows, :])

    def gate(r):
        rows = slice(r * rb, r * rb + rb)
        cg = us[rows, 1280:1792]
        gelu = 0.5 * cg * (1.0 + jnp.tanh(0.7978845608028654 * (cg + 0.044715 * (cg * cg * cg))))
        yc = gelu * b_ref[rows, :]
        ys[rows, 512:1024] = _rms(yc, gg_ref[:, 512:1024]).astype(BF16)

    mm_units = ([functools.partial(in_proj, c) for c in range(IN_WIDTH // MM_CHUNK)]
                + [functools.partial(out_proj, c) for c in range(D_MODEL // MM_CHUNK)])
    ew_units = ([(functools.partial(convs, r), 5.0) for r in range(n_rb)]
                + [(functools.partial(coeffs, r), 1.5) for r in range(n_rb)])
    ew_total = sum(cost for _, cost in ew_units) + 1.0 * n_rb
    n_mm_done, ew_done = 0, 0.0

    def emit_mm(upto):
        nonlocal n_mm_done
        while n_mm_done < len(mm_units) and n_mm_done < upto * len(mm_units):
            mm_units[n_mm_done]()
            n_mm_done += 1

    emit_mm(1.0 / len(mm_units))
    for unit, cost in ew_units:
        unit()
        ew_done += cost
        emit_mm(ew_done / ew_total)

    h = h_ref[...]
    for t in range(tt):
        rows = slice(t * nb, t * nb + nb)
        h = a_ref[rows, :] * h + b_ref[rows, :]
        b_ref[rows, :] = h

    for r in range(n_rb):
        gate(r)
        ew_done += 1.0
        emit_mm(ew_done / ew_total)
    emit_mm(1.0)

    @pl.when(jnp.logical_and(i >= 1, i <= n_tiles))
    def _():
        ta = pa_ref[tm:tm + ha, :]
        tb = gb_ref[tm:tm + hb, :]
        tc = cx_ref[tm:tm + hc, :]
        pa_ref[0:ha, :] = ta
        gb_ref[0:hb, :] = tb
        cx_ref[0:hc, :] = tc
        h_ref[...] = h


def _mixer_call(x, init_a, init_b, init_c, init_h, layer, w, *, nb, tm):
    m = x.shape[0]
    n_tiles = m // tm
    ha, hb, hc = (K_A - 1) * nb, (K_B - 1) * nb, (K_C - 1) * nb
    rb = max(nb, 64)
    gm = min(tm, 256)
    full = lambda shape: pl.BlockSpec(shape, lambda i: (0,) * len(shape))
    tile_new = pl.BlockSpec((tm, D_MODEL), lambda i: (jnp.minimum(i, n_tiles - 1), 0))
    tile_old = pl.BlockSpec((tm, D_MODEL), lambda i: (jnp.clip(i - 2, 0, n_tiles - 1), 0))
    return pl.pallas_call(
        functools.partial(_mixer_kernel, nb=nb, tm=tm, rb=rb, gm=gm, n_tiles=n_tiles),
        grid=(n_tiles + 2,),
        in_specs=[
            tile_new, tile_old,
            full((ha, W_A)), full((hb, W_B)), full((hc, W_C)), full((nb, W_C)),
            _const_spec((1, D_MODEL), layer),
            _const_spec((D_MODEL, IN_WIDTH), layer),
            _const_spec((K_A, W_A), layer),
            _const_spec((K_B, W_B), layer),
            _const_spec((1, W_B), layer),
            _const_spec((1, W_B), layer),
            _const_spec((1, W_B), layer),
            _const_spec((K_C, W_C), layer),
            _const_spec((1, W_C), layer),
            _const_spec((2, 256, 256), layer),
            _const_spec((1, W_C), layer),
            _const_spec((2, 256, 256), layer),
            _const_spec((1, W_C), layer),
            _const_spec((1, W_C), layer),
            _const_spec((1, D_MODEL), layer),
            _const_spec((D_MODEL, D_MODEL), layer),
        ],
        out_specs=[
            tile_old,
            full((ha, W_A)), full((hb, W_B)), full((hc, W_C)), full((nb, W_C)),
        ],
        out_shape=[
            jax.ShapeDtypeStruct((m, D_MODEL), F32),
            jax.ShapeDtypeStruct((ha, W_A), F32),
            jax.ShapeDtypeStruct((hb, W_B), F32),
            jax.ShapeDtypeStruct((hc, W_C), F32),
            jax.ShapeDtypeStruct((nb, W_C), F32),
        ],
        scratch_shapes=[
            pltpu.VMEM((tm, IN_WIDTH), F32),
            pltpu.VMEM((tm, IN_WIDTH), F32),
            pltpu.VMEM((ha + tm, W_A), F32),
            pltpu.VMEM((hb + tm, W_B), F32),
            pltpu.VMEM((hc + tm, W_C), F32),
            pltpu.VMEM((tm, W_C), F32),
            pltpu.VMEM((tm, W_C), BF16),
            pltpu.VMEM((tm, 2 * W_C), F32),
            pltpu.VMEM((tm, W_C), F32),
            pltpu.VMEM((tm, W_C), F32),
            pltpu.VMEM((nb, W_C), F32),
            pltpu.VMEM((tm, D_MODEL), BF16),
            pltpu.VMEM((tm, D_MODEL), BF16),
            pltpu.VMEM((tm, D_MODEL), BF16),
        ],
        compiler_params=pltpu.CompilerParams(
            dimension_semantics=("arbitrary",), vmem_limit_bytes=VMEM_LIMIT_BYTES),
        name="mixer",
    )(x, x, init_a, init_b, init_c, init_h, *w)


def _block_diag(w):
    half = LRU_HEADS // 2
    eye = jnp.eye(half, dtype=w.dtype)
    w = w.reshape(DEPTH, 2, half, LRU_HD, LRU_HD)
    full = jnp.einsum('lshij,hg->lshigj', w, eye)
    return full.reshape(DEPTH, 2, half * LRU_HD, half * LRU_HD).astype(BF16)


def _run_trunk(x, st_a, st_b, st_c, st_h, p, *, nb, tm_ffn, tm_mix):
    outs_a, outs_b, outs_c, outs_h = [], [], [], []
    for l in range(DEPTH):
        x = _ffn_call(x, l, p['g1'], p['wup1'], p['wd1'], p['gf'], tm=tm_ffn, final=False)
        x, oa, ob, oc, oh = _mixer_call(
            x, st_a[l].reshape(-1, W_A), st_b[l].reshape(-1, W_B), st_c[l].reshape(-1, W_C), st_h[l],
            l, p['mix'], nb=nb, tm=tm_mix)
        x = _ffn_call(x, l, p['g2'], p['wup2'], p['wd2'], p['gf'], tm=tm_ffn, final=(l == DEPTH - 1))
        outs_a.append(oa.reshape(K_A - 1, nb, W_A))
        outs_b.append(ob.reshape(K_B - 1, nb, W_B))
        outs_c.append(oc.reshape(K_C - 1, nb, W_C))
        outs_h.append(oh)
    to_seq_major = lambda s: jnp.stack(s).transpose(0, 2, 1, 3)
    return x, to_seq_major(outs_a), to_seq_major(outs_b), to_seq_major(outs_c), jnp.stack(outs_h)


def kernel(x_prompt, x_sample, state_conv_a, state_conv_b, state_conv_c, state_lru_h, norm_ffn1, w1_up, w1_down, norm_mix, w_in, conv_a_w, conv_b_w, conv_b_b, ln_b_g, ln_b_b, conv_c_w, conv_c_b, lru_wa, lru_ba, lru_wx, lru_bx, lru_lam, grp_g, w_out, norm_ffn2, w2_up, w2_down, final_norm):
    bp, seq, _ = x_prompt.shape
    bs = x_sample.shape[0]

    row = lambda v: v[:, None, :]
    p = {
        'g1': row(norm_ffn1), 'wup1': w1_up.astype(BF16), 'wd1': w1_down.astype(BF16),
        'g2': row(norm_ffn2), 'wup2': w2_up.astype(BF16), 'wd2': w2_down.astype(BF16),
        'gf': final_norm[None, :],
        'mix': (row(norm_mix), w_in.astype(BF16), conv_a_w, conv_b_w, row(conv_b_b), row(ln_b_g), row(ln_b_b),
                conv_c_w, row(conv_c_b), _block_diag(lru_wa), row(lru_ba), _block_diag(lru_wx), row(lru_bx),
                row(lru_lam), row(grp_g), w_out.astype(BF16)),
    }

    xp = x_prompt.transpose(1, 0, 2).reshape(seq * bp, D_MODEL)
    zeros = lambda k, c: jnp.zeros((DEPTH, k, bp, c), F32)
    yp, pa, pb, pc, ph = _run_trunk(xp, zeros(K_A - 1, W_A), zeros(K_B - 1, W_B), zeros(K_C - 1, W_C),
                                    jnp.zeros((DEPTH, bp, W_C), F32), p, nb=bp, tm_ffn=512, tm_mix=512)
    y_prompt = yp.reshape(seq, bp, D_MODEL).transpose(1, 0, 2)

    xs = x_sample.reshape(bs, D_MODEL)
    tmaj = lambda s: s.transpose(0, 2, 1, 3)
    ys, sa, sb, sc, sh = _run_trunk(xs, tmaj(state_conv_a), tmaj(state_conv_b), tmaj(state_conv_c),
                                    state_lru_h, p, nb=bs, tm_ffn=bs, tm_mix=bs)
    y_sample = ys.reshape(bs, 1, D_MODEL)
    return (y_prompt, y_sample, pa, pb, pc, ph, sa, sb, sc, sh)
```

```python
import functools

import jax
import jax.numpy as jnp
from jax import lax
from jax.experimental import pallas as pl
from jax.experimental.pallas import tpu as pltpu

D_MODEL = 1024
DEPTH = 4
W_A = 256
W_B = 256
W_C = 512
K_A = 3
K_B = 31
K_C = 4
LRU_HEADS = 8
LRU_HD = W_C // LRU_HEADS
LRU_C = 8.0
D_FF = 2816
RMS_EPS = 1e-6
LN_EPS = 1e-5
IN_WIDTH = 3 * W_A + 2 * W_B + 2 * W_C

FF_CHUNK = 256
N_FF_CHUNKS = D_FF // FF_CHUNK
MM_CHUNK = 256
SUBLANES = 8
VMEM_LIMIT_BYTES = 56 * 1024 * 1024

F32 = jnp.float32
BF16 = jnp.bfloat16


def _rms(x, g):
    return x * lax.rsqrt(jnp.mean(x * x, axis=-1, keepdims=True) + RMS_EPS) * g


def _sigmoid(x):
    return 1.0 / (1.0 + jnp.exp(-x))


def _const_spec(shape, layer):
    nd = len(shape)
    return pl.BlockSpec((None,) + tuple(shape), lambda i: (layer,) + (0,) * nd,
                        pipeline_mode=pl.Buffered(1))


def _ffn_kernel(x_ref, g_ref, wup_ref, wd_ref, gf_ref, o_ref, *, final, layout, nb):
    if layout == 'from_seq':
        tq = x_ref.shape[1]
        x = x_ref[...].reshape(nb * tq, D_MODEL)
    elif layout == 'to_seq':
        tq = x_ref.shape[0] // nb
        x = jnp.swapaxes(x_ref[...].reshape(tq, nb, D_MODEL), 0, 1).reshape(nb * tq, D_MODEL)
    else:
        x = x_ref[...]
    h = _rms(x, g_ref[...]).astype(BF16)
    acc = jnp.zeros(x.shape, F32)
    for c in range(N_FF_CHUNKS):
        lo = c * FF_CHUNK
        wg = wup_ref[:, lo:lo + FF_CHUNK].astype(BF16)
        wu = wup_ref[:, D_FF + lo:D_FF + lo + FF_CHUNK].astype(BF16)
        wd = wd_ref[lo:lo + FF_CHUNK, :].astype(BF16)
        g = jnp.dot(h, wg, preferred_element_type=F32)
        u = jnp.dot(h, wu, preferred_element_type=F32)
        a = (g * _sigmoid(g) * u).astype(BF16)
        acc = acc + jnp.dot(a, wd, preferred_element_type=F32)
    y = x + 0.5 * acc
    if final:
        y = _rms(y, gf_ref[...])
    if layout == 'from_seq':
        o_ref[...] = jnp.swapaxes(y.reshape(nb, tq, D_MODEL), 0, 1).reshape(tq * nb, D_MODEL)
    elif layout == 'to_seq':
        o_ref[...] = y.reshape(nb, tq, D_MODEL)
    else:
        o_ref[...] = y


def _ffn_call(x, layer, norm_g, wup, wd, final_g, *, tm, final, layout='rows', nb=1):
    m = x.shape[0] * x.shape[1] if layout == 'from_seq' else x.shape[0]
    rows_spec = pl.BlockSpec((tm, D_MODEL), lambda i: (i, 0))
    seq_spec = pl.BlockSpec((nb, tm // nb, D_MODEL), lambda i: (0, i, 0))
    return pl.pallas_call(
        functools.partial(_ffn_kernel, final=final, layout=layout, nb=nb),
        grid=(m // tm,),
        in_specs=[
            seq_spec if layout == 'from_seq' else rows_spec,
            _const_spec((1, D_MODEL), layer),
            _const_spec((D_MODEL, 2 * D_FF), layer),
            _const_spec((D_FF, D_MODEL), layer),
            pl.BlockSpec((1, D_MODEL), lambda i: (0, 0)),
        ],
        out_specs=seq_spec if layout == 'to_seq' else rows_spec,
        out_shape=jax.ShapeDtypeStruct((nb, m // nb, D_MODEL) if layout == 'to_seq' else (m, D_MODEL), F32),
        compiler_params=pltpu.CompilerParams(
            dimension_semantics=("arbitrary",), vmem_limit_bytes=VMEM_LIMIT_BYTES),
        name="ffn",
    )(x, norm_g, wup, wd, final_g)


def _mixer_kernel(x_ref, xr_ref, ia_ref, ib_ref, ic_ref, ih_ref,
                  gmix_ref, win_ref, caw_ref, cbw_ref, cbb_ref, lng_ref, lnb_ref,
                  ccw_ref, ccb_ref, wa_ref, ba_ref, wx_ref, bx_ref, lam_ref, gg_ref, wout_ref,
                  o_ref, oa_ref, ob_ref, oc_ref, oh_ref,
                  u0_ref, u1_ref, pa_ref, gb_ref, cx_ref, xc_ref, xcb_ref, g_ref, a_ref, b_ref, h_ref,
                  y0_ref, y1_ref, hx_ref, *, nb, tm, rb, gm, n_tiles):
    i = pl.program_id(0)
    ha, hb, hc = (K_A - 1) * nb, (K_B - 1) * nb, (K_C - 1) * nb
    pipelined = n_tiles > 1

    @pl.when(i == 0)
    def _():
        pa_ref[0:ha, :] = ia_ref[...]
        gb_ref[0:hb, :] = ib_ref[...]
        cx_ref[0:hc, :] = ic_ref[...]
        h_ref[...] = ih_ref[...]
        if pipelined:
            u1_ref[...] = jnp.zeros((tm, IN_WIDTH), F32)
            y0_ref[...] = jnp.zeros((tm, D_MODEL), BF16)
            y1_ref[...] = jnp.zeros((tm, D_MODEL), BF16)

    step = functools.partial(
        _mixer_step, x_ref, xr_ref, gmix_ref, win_ref, caw_ref, cbw_ref, cbb_ref, lng_ref, lnb_ref,
        ccw_ref, ccb_ref, wa_ref, ba_ref, wx_ref, bx_ref, lam_ref, gg_ref, wout_ref, o_ref,
        pa_ref, gb_ref, cx_ref, xc_ref, xcb_ref, g_ref, a_ref, b_ref, h_ref, hx_ref,
        nb=nb, tm=tm, rb=rb, gm=gm, n_tiles=n_tiles, pipelined=pipelined)

    if pipelined:
        @pl.when(i % 2 == 0)
        def _():
            step(u0_ref, u1_ref, y0_ref, y1_ref)

        @pl.when(i % 2 == 1)
        def _():
            step(u1_ref, u0_ref, y1_ref, y0_ref)
    else:
        step(u0_ref, u0_ref, y0_ref, y0_ref)

    @pl.when(i == pl.num_programs(0) - 1)
    def _():
        oa_ref[...] = pa_ref[0:ha, :]
        ob_ref[...] = gb_ref[0:hb, :]
        oc_ref[...] = cx_ref[0:hc, :]
        oh_ref[...] = h_ref[...]


def _mixer_step(x_ref, xr_ref, gmix_ref, win_ref, caw_ref, cbw_ref, cbb_ref, lng_ref, lnb_ref,
                ccw_ref, ccb_ref, wa_ref, ba_ref, wx_ref, bx_ref, lam_ref, gg_ref, wout_ref, o_ref,
                pa_ref, gb_ref, cx_ref, xc_ref, xcb_ref, g_ref, a_ref, b_ref, h_ref, hx_ref,
                u_new, us, y_old, ys, *, nb, tm, rb, gm, n_tiles, pipelined):
    i = pl.program_id(0)
    ha, hb, hc = (K_A - 1) * nb, (K_B - 1) * nb, (K_C - 1) * nb
    n_rb = tm // rb
    tt = tm // nb

    hx_ref[...] = _rms(x_ref[...], gmix_ref[...]).astype(BF16)

    def in_proj(c):
        cols = slice(c * MM_CHUNK, (c + 1) * MM_CHUNK)
        u_new[:, cols] = jnp.dot(hx_ref[...], win_ref[:, cols], preferred_element_type=F32)

    def out_proj(c):
        cols = slice(c * MM_CHUNK, (c + 1) * MM_CHUNK)
        o_ref[:, cols] = xr_ref[:, cols] + jnp.dot(y_old[...], wout_ref[:, cols],
                                                   preferred_element_type=F32)

    sp = lam_ref[...]
    sp = jnp.maximum(-sp, 0.0) + jnp.log1p(jnp.exp(-jnp.abs(sp)))

    def convs(r):
        r0 = r * rb
        rows = slice(r0, r0 + rb)
        pa_ref[ha + r0:ha + r0 + rb, :] = us[rows, 256:512] * us[rows, 512:768]
        gb_ref[hb + r0:hb + r0 + rb, :] = us[rows, 768:1024] * _sigmoid(us[rows, 1024:1280])
        cx_ref[hc + r0:hc + r0 + rb, :] = us[rows, 1792:2304]
        ca = pa_ref[r0:r0 + rb, :] * caw_ref[0:1, :]
        for k in range(1, K_A):
            ca = ca + pa_ref[r0 + k * nb:r0 + k * nb + rb, :] * caw_ref[k:k + 1, :]
        ya = us[rows, 0:256] * ca
        ys[rows, 0:256] = _rms(ya, gg_ref[:, 0:256]).astype(BF16)
        cb = gb_ref[r0:r0 + rb, :] * cbw_ref[0:1, :]
        for k in range(1, K_B):
            cb = cb + gb_ref[r0 + k * nb:r0 + k * nb + rb, :] * cbw_ref[k:k + 1, :]
        cb = cb + cbb_ref[...]
        mu = jnp.mean(cb, axis=-1, keepdims=True)
        cc = cb - mu
        var = jnp.mean(cc * cc, axis=-1, keepdims=True)
        ln = cc * lax.rsqrt(var + LN_EPS) * lng_ref[...] + lnb_ref[...]
        yb = ln * _sigmoid(ln)
        ys[rows, 256:512] = _rms(yb, gg_ref[:, 256:512]).astype(BF16)
        xc = cx_ref[r0:r0 + rb, :] * ccw_ref[0:1, :]
        for k in range(1, K_C):
            xc = xc + cx_ref[r0 + k * nb:r0 + k * nb + rb, :] * ccw_ref[k:k + 1, :]
        xc = xc + ccb_ref[...]
        xc_ref[rows, :] = xc
        xcb_ref[rows, :] = xc.astype(BF16)
        if (r0 + rb) % gm == 0:
            grows = slice(r0 + rb - gm, r0 + rb)
            for half in range(2):
                cols = slice(half * 256, half * 256 + 256)
                g_ref[grows, cols] = jnp.dot(xcb_ref[grows, cols], wa_ref[half],
                                             preferred_element_type=F32)
                g_ref[grows, 512 + half * 256:768 + half * 256] = jnp.dot(
                    xcb_ref[grows, cols], wx_ref[half], preferred_element_type=F32)

    def coeffs(r):
        rows = slice(r * rb, r * rb + rb)
        rg = _sigmoid(g_ref[rows, 0:512] + ba_ref[...])
        ig = _sigmoid(g_ref[rows, 512:1024] + bx_ref[...])
        log_a = (-LRU_C) * rg * sp
        a = jnp.exp(log_a)
        gain = jnp.sqrt(jnp.maximum(jnp.tanh(-log_a) * (1.0 + a * a), 0.0))
        a_ref[rows, :] = a
        b_ref[rows, :] = gain * (ig * xc_ref[rows, :])

    def gate(r):
        rows = slice(r * rb, r * rb + rb)
        cg = us[rows, 1280:1792]
        gelu = 0.5 * cg * (1.0 + jnp.tanh(0.7978845608028654 * (cg + 0.044715 * (cg * cg * cg))))
        yc = gelu * b_ref[rows, :]
        ys[rows, 512:1024] = _rms(yc, gg_ref[:, 512:1024]).astype(BF16)

    mm_units = ([functools.partial(in_proj, c) for c in range(IN_WIDTH // MM_CHUNK)]
                + [functools.partial(out_proj, c) for c in range(D_MODEL // MM_CHUNK)])
    ew_units = ([(functools.partial(convs, r), 5.0) for r in range(n_rb)]
                + [(functools.partial(coeffs, r), 1.5) for r in range(n_rb)])
    ew_total = sum(cost for _, cost in ew_units) + 1.0 * n_rb
    n_mm_done, ew_done = 0, 0.0

    n_in_proj = IN_WIDTH // MM_CHUNK

    def emit_mm(upto):
        nonlocal n_mm_done
        if pipelined:
            target = min(len(mm_units), int(upto * len(mm_units) + 0.999))
        else:
            target = n_in_proj if upto < 1.0 else len(mm_units)
        while n_mm_done < target:
            mm_units[n_mm_done]()
            n_mm_done += 1

    emit_mm(1.0 / len(mm_units))
    for unit, cost in ew_units:
        unit()
        ew_done += cost
        emit_mm(ew_done / ew_total)

    h = h_ref[...]
    for t in range(tt):
        rows = slice(t * nb, t * nb + nb)
        h = a_ref[rows, :] * h + b_ref[rows, :]
        b_ref[rows, :] = h

    for r in range(n_rb):
        gate(r)
        ew_done += 1.0
        emit_mm(ew_done / ew_total)
    emit_mm(1.0)

    @pl.when(jnp.logical_and(i >= 1, i <= n_tiles) if pipelined else i == 0)
    def _():
        ta = pa_ref[tm:tm + ha, :]
        tb = gb_ref[tm:tm + hb, :]
        tc = cx_ref[tm:tm + hc, :]
        pa_ref[0:ha, :] = ta
        gb_ref[0:hb, :] = tb
        cx_ref[0:hc, :] = tc
        h_ref[...] = h


def _mixer_call(x, init_a, init_b, init_c, init_h, layer, w, *, nb, tm):
    m = x.shape[0]
    n_tiles = m // tm
    ha, hb, hc = (K_A - 1) * nb, (K_B - 1) * nb, (K_C - 1) * nb
    rb = max(nb, 64)
    gm = min(tm, 256)
    full = lambda shape: pl.BlockSpec(shape, lambda i: (0,) * len(shape))
    tile_new = pl.BlockSpec((tm, D_MODEL), lambda i: (jnp.minimum(i, n_tiles - 1), 0))
    tile_old = pl.BlockSpec((tm, D_MODEL), lambda i: (jnp.clip(i - 2, 0, n_tiles - 1), 0))
    return pl.pallas_call(
        functools.partial(_mixer_kernel, nb=nb, tm=tm, rb=rb, gm=gm, n_tiles=n_tiles),
        grid=(n_tiles + 2 if n_tiles > 1 else 1,),
        in_specs=[
            tile_new, tile_old,
            full((ha, W_A)), full((hb, W_B)), full((hc, W_C)), full((nb, W_C)),
            _const_spec((1, D_MODEL), layer),
            _const_spec((D_MODEL, IN_WIDTH), layer),
            _const_spec((K_A, W_A), layer),
            _const_spec((K_B, W_B), layer),
            _const_spec((1, W_B), layer),
            _const_spec((1, W_B), layer),
            _const_spec((1, W_B), layer),
            _const_spec((K_C, W_C), layer),
            _const_spec((1, W_C), layer),
            _const_spec((2, 256, 256), layer),
            _const_spec((1, W_C), layer),
            _const_spec((2, 256, 256), layer),
            _const_spec((1, W_C), layer),
            _const_spec((1, W_C), layer),
            _const_spec((1, D_MODEL), layer),
            _const_spec((D_MODEL, D_MODEL), layer),
        ],
        out_specs=[
            tile_old,
            full((ha, W_A)), full((hb, W_B)), full((hc, W_C)), full((nb, W_C)),
        ],
        out_shape=[
            jax.ShapeDtypeStruct((m, D_MODEL), F32),
            jax.ShapeDtypeStruct((ha, W_A), F32),
            jax.ShapeDtypeStruct((hb, W_B), F32),
            jax.ShapeDtypeStruct((hc, W_C), F32),
            jax.ShapeDtypeStruct((nb, W_C), F32),
        ],
        scratch_shapes=[
            pltpu.VMEM((tm, IN_WIDTH), F32),
            pltpu.VMEM((tm, IN_WIDTH), F32),
            pltpu.VMEM((ha + tm, W_A), F32),
            pltpu.VMEM((hb + tm, W_B), F32),
            pltpu.VMEM((hc + tm, W_C), F32),
            pltpu.VMEM((tm, W_C), F32),
            pltpu.VMEM((tm, W_C), BF16),
            pltpu.VMEM((tm, 2 * W_C), F32),
            pltpu.VMEM((tm, W_C), F32),
            pltpu.VMEM((tm, W_C), F32),
            pltpu.VMEM((nb, W_C), F32),
            pltpu.VMEM((tm, D_MODEL), BF16),
            pltpu.VMEM((tm, D_MODEL), BF16),
            pltpu.VMEM((tm, D_MODEL), BF16),
        ],
        compiler_params=pltpu.CompilerParams(
            dimension_semantics=("arbitrary",), vmem_limit_bytes=VMEM_LIMIT_BYTES),
        name="mixer",
    )(x, x, init_a, init_b, init_c, init_h, *w)


def _block_diag(w):
    half = LRU_HEADS // 2
    eye = jnp.eye(half, dtype=w.dtype)
    w = w.reshape(DEPTH, 2, half, LRU_HD, LRU_HD)
    full = jnp.einsum('lshij,hg->lshigj', w, eye)
    return full.reshape(DEPTH, 2, half * LRU_HD, half * LRU_HD).astype(BF16)


def _run_trunk(x, st_a, st_b, st_c, st_h, p, *, nb, tm_ffn, tm_mix, seq_major):
    outs_a, outs_b, outs_c, outs_h = [], [], [], []
    for l in range(DEPTH):
        first, last = seq_major and l == 0, seq_major and l == DEPTH - 1
        x = _ffn_call(x, l, p['g1'], p['wup1'], p['wd1'], p['gf'], tm=tm_ffn, final=False,
                      layout='from_seq' if first else 'rows', nb=nb)
        x, oa, ob, oc, oh = _mixer_call(
            x, st_a[l].reshape(-1, W_A), st_b[l].reshape(-1, W_B), st_c[l].reshape(-1, W_C), st_h[l],
            l, p['mix'], nb=nb, tm=tm_mix)
        x = _ffn_call(x, l, p['g2'], p['wup2'], p['wd2'], p['gf'], tm=tm_ffn, final=(l == DEPTH - 1),
                      layout='to_seq' if last else 'rows', nb=nb)
        outs_a.append(oa.reshape(K_A - 1, nb, W_A))
        outs_b.append(ob.reshape(K_B - 1, nb, W_B))
        outs_c.append(oc.reshape(K_C - 1, nb, W_C))
        outs_h.append(oh)
    to_seq_major = lambda s: jnp.stack(s).transpose(0, 2, 1, 3)
    return x, to_seq_major(outs_a), to_seq_major(outs_b), to_seq_major(outs_c), jnp.stack(outs_h)


def kernel(x_prompt, x_sample, state_conv_a, state_conv_b, state_conv_c, state_lru_h, norm_ffn1, w1_up, w1_down, norm_mix, w_in, conv_a_w, conv_b_w, conv_b_b, ln_b_g, ln_b_b, conv_c_w, conv_c_b, lru_wa, lru_ba, lru_wx, lru_bx, lru_lam, grp_g, w_out, norm_ffn2, w2_up, w2_down, final_norm):
    bp, seq, _ = x_prompt.shape
    bs = x_sample.shape[0]

    row = lambda v: v[:, None, :]
    p = {
        'g1': row(norm_ffn1), 'wup1': w1_up, 'wd1': w1_down,
        'g2': row(norm_ffn2), 'wup2': w2_up, 'wd2': w2_down,
        'gf': final_norm[None, :],
        'mix': (row(norm_mix), w_in.astype(BF16), conv_a_w, conv_b_w, row(conv_b_b), row(ln_b_g), row(ln_b_b),
                conv_c_w, row(conv_c_b), _block_diag(lru_wa), row(lru_ba), _block_diag(lru_wx), row(lru_bx),
                row(lru_lam), row(grp_g), w_out.astype(BF16)),
    }

    zeros = lambda k, c: jnp.zeros((DEPTH, k, bp, c), F32)
    y_prompt, pa, pb, pc, ph = _run_trunk(
        x_prompt, zeros(K_A - 1, W_A), zeros(K_B - 1, W_B), zeros(K_C - 1, W_C),
        jnp.zeros((DEPTH, bp, W_C), F32), p, nb=bp, tm_ffn=512, tm_mix=512, seq_major=True)

    xs = x_sample.reshape(bs, D_MODEL)
    tmaj = lambda s: s.transpose(0, 2, 1, 3)
    ys, sa, sb, sc, sh = _run_trunk(xs, tmaj(state_conv_a), tmaj(state_conv_b), tmaj(state_conv_c),
                                    state_lru_h, p, nb=bs, tm_ffn=bs, tm_mix=bs, seq_major=False)
    y_sample = ys.reshape(bs, 1, D_MODEL)
    return (y_prompt, y_sample, pa, pb, pc, ph, sa, sb, sc, sh)
```

```python
import functools

import jax
import jax.numpy as jnp
from jax import lax
from jax.experimental import pallas as pl
from jax.experimental.pallas import tpu as pltpu

D_MODEL = 1024
DEPTH = 4
W_A = 256
W_B = 256
W_C = 512
K_A = 3
K_B = 31
K_C = 4
LRU_HEADS = 8
LRU_HD = W_C // LRU_HEADS
LRU_C = 8.0
D_FF = 2816
RMS_EPS = 1e-6
LN_EPS = 1e-5
IN_WIDTH = 3 * W_A + 2 * W_B + 2 * W_C

FF_CHUNK = 256
N_FF_CHUNKS = D_FF // FF_CHUNK
MM_CHUNK = 256
TM_PROMPT = 512
SUBLANES = 8
VMEM_LIMIT_BYTES = 56 * 1024 * 1024

F32 = jnp.float32
BF16 = jnp.bfloat16


def _rms(x, g):
    return x * lax.rsqrt(jnp.mean(x * x, axis=-1, keepdims=True) + RMS_EPS) * g


def _sigmoid(x):
    return 1.0 / (1.0 + jnp.exp(-x))


def _const_spec(shape, layer):
    nd = len(shape)
    return pl.BlockSpec((None,) + tuple(shape), lambda i: (layer,) + (0,) * nd,
                        pipeline_mode=pl.Buffered(1))


def _swiglu_half_step(x, g_ref, wup_ref, wd_ref, gf_ref, final):
    h = _rms(x, g_ref[...]).astype(BF16)
    acc = jnp.zeros(x.shape, F32)
    for c in range(N_FF_CHUNKS):
        lo = c * FF_CHUNK
        wg = wup_ref[:, lo:lo + FF_CHUNK].astype(BF16)
        wu = wup_ref[:, D_FF + lo:D_FF + lo + FF_CHUNK].astype(BF16)
        wd = wd_ref[lo:lo + FF_CHUNK, :].astype(BF16)
        g = jnp.dot(h, wg, preferred_element_type=F32)
        u = jnp.dot(h, wu, preferred_element_type=F32)
        a = (g * _sigmoid(g) * u).astype(BF16)
        acc = acc + jnp.dot(a, wd, preferred_element_type=F32)
    y = x + 0.5 * acc
    if final:
        y = _rms(y, gf_ref[...])
    return y


def _ffn_kernel(x_ref, xs_ref, g_ref, wup_ref, wd_ref, gf_ref, o_ref, os_ref, *, final, layout, nb):
    if layout == 'from_seq':
        tq = x_ref.shape[1]
        x = x_ref[...].reshape(nb * tq, D_MODEL)
    elif layout == 'to_seq':
        tq = x_ref.shape[0] // nb
        x = jnp.swapaxes(x_ref[...].reshape(tq, nb, D_MODEL), 0, 1).reshape(nb * tq, D_MODEL)
    else:
        x = x_ref[...]
    y = _swiglu_half_step(x, g_ref, wup_ref, wd_ref, gf_ref, final)
    if layout == 'from_seq':
        o_ref[...] = jnp.swapaxes(y.reshape(nb, tq, D_MODEL), 0, 1).reshape(tq * nb, D_MODEL)
    elif layout == 'to_seq':
        o_ref[...] = y.reshape(nb, tq, D_MODEL)
    else:
        o_ref[...] = y

    @pl.when(pl.program_id(0) == pl.num_programs(0) - 1)
    def _():
        os_ref[...] = _swiglu_half_step(xs_ref[...], g_ref, wup_ref, wd_ref, gf_ref, final)


def _ffn_call(x, xs, layer, norm_g, wup, wd, final_g, *, tm, final, layout='rows', nb=1):
    m = x.shape[0] * x.shape[1] if layout == 'from_seq' else x.shape[0]
    ms = xs.shape[0]
    rows_spec = pl.BlockSpec((tm, D_MODEL), lambda i: (i, 0))
    seq_spec = pl.BlockSpec((nb, tm // nb, D_MODEL), lambda i: (0, i, 0))
    sample_spec = pl.BlockSpec((ms, D_MODEL), lambda i: (0, 0))
    return pl.pallas_call(
        functools.partial(_ffn_kernel, final=final, layout=layout, nb=nb),
        grid=(m // tm,),
        in_specs=[
            seq_spec if layout == 'from_seq' else rows_spec,
            sample_spec,
            _const_spec((1, D_MODEL), layer),
            _const_spec((D_MODEL, 2 * D_FF), layer),
            _const_spec((D_FF, D_MODEL), layer),
            pl.BlockSpec((1, D_MODEL), lambda i: (0, 0)),
        ],
        out_specs=[seq_spec if layout == 'to_seq' else rows_spec, sample_spec],
        out_shape=[
            jax.ShapeDtypeStruct((nb, m // nb, D_MODEL) if layout == 'to_seq' else (m, D_MODEL), F32),
            jax.ShapeDtypeStruct((ms, D_MODEL), F32),
        ],
        compiler_params=pltpu.CompilerParams(
            dimension_semantics=("arbitrary",), vmem_limit_bytes=VMEM_LIMIT_BYTES),
        name="ffn",
    )(x, xs, norm_g, wup, wd, final_g)


def _mixer_kernel(x_ref, xr_ref, ia_ref, ib_ref, ic_ref, ih_ref,
                  gmix_ref, win_ref, caw_ref, cbw_ref, cbb_ref, lng_ref, lnb_ref,
                  ccw_ref, ccb_ref, wa_ref, ba_ref, wx_ref, bx_ref, lam_ref, gg_ref, wout_ref,
                  o_ref, oa_ref, ob_ref, oc_ref, oh_ref,
                  u0_ref, u1_ref, pa_ref, gb_ref, cx_ref, xc_ref, xcb_ref, g_ref, a_ref, b_ref, h_ref,
                  y0_ref, y1_ref, hx_ref, *, nb, tm, rb, gm, n_tiles):
    i = pl.program_id(0)
    ha, hb, hc = (K_A - 1) * nb, (K_B - 1) * nb, (K_C - 1) * nb
    pipelined = n_tiles > 1

    @pl.when(i == 0)
    def _():
        pa_ref[0:ha, :] = ia_ref[...]
        gb_ref[0:hb, :] = ib_ref[...]
        cx_ref[0:hc, :] = ic_ref[...]
        h_ref[...] = ih_ref[...]
        if pipelined:
            u1_ref[...] = jnp.zeros((tm, IN_WIDTH), F32)
            y0_ref[...] = jnp.zeros((tm, D_MODEL), BF16)
            y1_ref[...] = jnp.zeros((tm, D_MODEL), BF16)

    step = functools.partial(
        _mixer_step, x_ref, xr_ref, gmix_ref, win_ref, caw_ref, cbw_ref, cbb_ref, lng_ref, lnb_ref,
        ccw_ref, ccb_ref, wa_ref, ba_ref, wx_ref, bx_ref, lam_ref, gg_ref, wout_ref, o_ref,
        pa_ref, gb_ref, cx_ref, xc_ref, xcb_ref, g_ref, a_ref, b_ref, h_ref, hx_ref,
        nb=nb, tm=tm, rb=rb, gm=gm, n_tiles=n_tiles, pipelined=pipelined)

    if pipelined:
        @pl.when(i % 2 == 0)
        def _():
            step(u0_ref, u1_ref, y0_ref, y1_ref)

        @pl.when(i % 2 == 1)
        def _():
            step(u1_ref, u0_ref, y1_ref, y0_ref)
    else:
        step(u0_ref, u0_ref, y0_ref, y0_ref)

    @pl.when(i == pl.num_programs(0) - 1)
    def _():
        oa_ref[...] = pa_ref[0:ha, :]
        ob_ref[...] = gb_ref[0:hb, :]
        oc_ref[...] = cx_ref[0:hc, :]
        oh_ref[...] = h_ref[...]


def _mixer_step(x_ref, xr_ref, gmix_ref, win_ref, caw_ref, cbw_ref, cbb_ref, lng_ref, lnb_ref,
                ccw_ref, ccb_ref, wa_ref, ba_ref, wx_ref, bx_ref, lam_ref, gg_ref, wout_ref, o_ref,
                pa_ref, gb_ref, cx_ref, xc_ref, xcb_ref, g_ref, a_ref, b_ref, h_ref, hx_ref,
                u_new, us, y_old, ys, *, nb, tm, rb, gm, n_tiles, pipelined):
    i = pl.program_id(0)
    ha, hb, hc = (K_A - 1) * nb, (K_B - 1) * nb, (K_C - 1) * nb
    n_rb = tm // rb
    tt = tm // nb

    hx_ref[...] = _rms(x_ref[...], gmix_ref[...]).astype(BF16)

    def in_proj(c):
        cols = slice(c * MM_CHUNK, (c + 1) * MM_CHUNK)
        u_new[:, cols] = jnp.dot(hx_ref[...], win_ref[:, cols], preferred_element_type=F32)

    def out_proj(c):
        cols = slice(c * MM_CHUNK, (c + 1) * MM_CHUNK)
        o_ref[:, cols] = xr_ref[:, cols] + jnp.dot(y_old[...], wout_ref[:, cols],
                                                   preferred_element_type=F32)

    sp = lam_ref[...]
    sp = jnp.maximum(-sp, 0.0) + jnp.log1p(jnp.exp(-jnp.abs(sp)))

    def convs(r):
        r0 = r * rb
        rows = slice(r0, r0 + rb)
        pa_ref[ha + r0:ha + r0 + rb, :] = us[rows, 256:512] * us[rows, 512:768]
        gb_ref[hb + r0:hb + r0 + rb, :] = us[rows, 768:1024] * _sigmoid(us[rows, 1024:1280])
        cx_ref[hc + r0:hc + r0 + rb, :] = us[rows, 1792:2304]
        ca = pa_ref[r0:r0 + rb, :] * caw_ref[0:1, :]
        for k in range(1, K_A):
            ca = ca + pa_ref[r0 + k * nb:r0 + k * nb + rb, :] * caw_ref[k:k + 1, :]
        ya = us[rows, 0:256] * ca
        ys[rows, 0:256] = _rms(ya, gg_ref[:, 0:256]).astype(BF16)
        cb = gb_ref[r0:r0 + rb, :] * cbw_ref[0:1, :]
        for k in range(1, K_B):
            cb = cb + gb_ref[r0 + k * nb:r0 + k * nb + rb, :] * cbw_ref[k:k + 1, :]
        cb = cb + cbb_ref[...]
        mu = jnp.mean(cb, axis=-1, keepdims=True)
        cc = cb - mu
        var = jnp.mean(cc * cc, axis=-1, keepdims=True)
        ln = cc * lax.rsqrt(var + LN_EPS) * lng_ref[...] + lnb_ref[...]
        yb = ln * _sigmoid(ln)
        ys[rows, 256:512] = _rms(yb, gg_ref[:, 256:512]).astype(BF16)
        xc = cx_ref[r0:r0 + rb, :] * ccw_ref[0:1, :]
        for k in range(1, K_C):
            xc = xc + cx_ref[r0 + k * nb:r0 + k * nb + rb, :] * ccw_ref[k:k + 1, :]
        xc = xc + ccb_ref[...]
        xc_ref[rows, :] = xc
        xcb_ref[rows, :] = xc.astype(BF16)
        if (r0 + rb) % gm == 0:
            grows = slice(r0 + rb - gm, r0 + rb)
            for half in range(2):
                cols = slice(half * 256, half * 256 + 256)
                g_ref[grows, cols] = jnp.dot(xcb_ref[grows, cols], wa_ref[half],
                                             preferred_element_type=F32)
                g_ref[grows, 512 + half * 256:768 + half * 256] = jnp.dot(
                    xcb_ref[grows, cols], wx_ref[half], preferred_element_type=F32)

    def coeffs(r):
        rows = slice(r * rb, r * rb + rb)
        rg = _sigmoid(g_ref[rows, 0:512] + ba_ref[...])
        ig = _sigmoid(g_ref[rows, 512:1024] + bx_ref[...])
        log_a = (-LRU_C) * rg * sp
        a = jnp.exp(log_a)
        gain = jnp.sqrt(jnp.maximum(jnp.tanh(-log_a) * (1.0 + a * a), 0.0))
        a_ref[rows, :] = a
        b_ref[rows, :] = gain * (ig * xc_ref[rows, :])

    def gate(r):
        rows = slice(r * rb, r * rb + rb)
        cg = us[rows, 1280:1792]
        gelu = 0.5 * cg * (1.0 + jnp.tanh(0.7978845608028654 * (cg + 0.044715 * (cg * cg * cg))))
        yc = gelu * b_ref[rows, :]
        ys[rows, 512:1024] = _rms(yc, gg_ref[:, 512:1024]).astype(BF16)

    mm_units = ([functools.partial(in_proj, c) for c in range(IN_WIDTH // MM_CHUNK)]
                + [functools.partial(out_proj, c) for c in range(D_MODEL // MM_CHUNK)])
    ew_units = ([(functools.partial(convs, r), 5.0) for r in range(n_rb)]
                + [(functools.partial(coeffs, r), 1.5) for r in range(n_rb)])
    ew_total = sum(cost for _, cost in ew_units) + 1.0 * n_rb
    n_mm_done, ew_done = 0, 0.0

    n_in_proj = IN_WIDTH // MM_CHUNK

    def emit_mm(upto):
        nonlocal n_mm_done
        if pipelined:
            target = min(len(mm_units), int(upto * len(mm_units) + 0.999))
        else:
            target = n_in_proj if upto < 1.0 else len(mm_units)
        while n_mm_done < target:
            mm_units[n_mm_done]()
            n_mm_done += 1

    emit_mm(1.0 / len(mm_units))
    for unit, cost in ew_units:
        unit()
        ew_done += cost
        emit_mm(ew_done / ew_total)

    h = h_ref[...]
    for t in range(tt):
        rows = slice(t * nb, t * nb + nb)
        h = a_ref[rows, :] * h + b_ref[rows, :]
        b_ref[rows, :] = h

    for r in range(n_rb):
        gate(r)
        ew_done += 1.0
        emit_mm(ew_done / ew_total)
    emit_mm(1.0)

    @pl.when(jnp.logical_and(i >= 1, i <= n_tiles) if pipelined else i == 0)
    def _():
        ta = pa_ref[tm:tm + ha, :]
        tb = gb_ref[tm:tm + hb, :]
        tc = cx_ref[tm:tm + hc, :]
        pa_ref[0:ha, :] = ta
        gb_ref[0:hb, :] = tb
        cx_ref[0:hc, :] = tc
        h_ref[...] = h


def _mixer_call(x, init_a, init_b, init_c, init_h, layer, w, *, nb, tm):
    m = x.shape[0]
    n_tiles = m // tm
    ha, hb, hc = (K_A - 1) * nb, (K_B - 1) * nb, (K_C - 1) * nb
    rb = max(nb, 64)
    gm = min(tm, 256)
    full = lambda shape: pl.BlockSpec(shape, lambda i: (0,) * len(shape))
    tile_new = pl.BlockSpec((tm, D_MODEL), lambda i: (jnp.minimum(i, n_tiles - 1), 0))
    tile_old = pl.BlockSpec((tm, D_MODEL), lambda i: (jnp.clip(i - 2, 0, n_tiles - 1), 0))
    return pl.pallas_call(
        functools.partial(_mixer_kernel, nb=nb, tm=tm, rb=rb, gm=gm, n_tiles=n_tiles),
        grid=(n_tiles + 2 if n_tiles > 1 else 1,),
        in_specs=[
            tile_new, tile_old,
            full((ha, W_A)), full((hb, W_B)), full((hc, W_C)), full((nb, W_C)),
            _const_spec((1, D_MODEL), layer),
            _const_spec((D_MODEL, IN_WIDTH), layer),
            _const_spec((K_A, W_A), layer),
            _const_spec((K_B, W_B), layer),
            _const_spec((1, W_B), layer),
            _const_spec((1, W_B), layer),
            _const_spec((1, W_B), layer),
            _const_spec((K_C, W_C), layer),
            _const_spec((1, W_C), layer),
            _const_spec((2, 256, 256), layer),
            _const_spec((1, W_C), layer),
            _const_spec((2, 256, 256), layer),
            _const_spec((1, W_C), layer),
            _const_spec((1, W_C), layer),
            _const_spec((1, D_MODEL), layer),
            _const_spec((D_MODEL, D_MODEL), layer),
        ],
        out_specs=[
            tile_old,
            full((ha, W_A)), full((hb, W_B)), full((hc, W_C)), full((nb, W_C)),
        ],
        out_shape=[
            jax.ShapeDtypeStruct((m, D_MODEL), F32),
            jax.ShapeDtypeStruct((ha, W_A), F32),
            jax.ShapeDtypeStruct((hb, W_B), F32),
            jax.ShapeDtypeStruct((hc, W_C), F32),
            jax.ShapeDtypeStruct((nb, W_C), F32),
        ],
        scratch_shapes=[
            pltpu.VMEM((tm, IN_WIDTH), F32),
            pltpu.VMEM((tm, IN_WIDTH), F32),
            pltpu.VMEM((ha + tm, W_A), F32),
            pltpu.VMEM((hb + tm, W_B), F32),
            pltpu.VMEM((hc + tm, W_C), F32),
            pltpu.VMEM((tm, W_C), F32),
            pltpu.VMEM((tm, W_C), BF16),
            pltpu.VMEM((tm, 2 * W_C), F32),
            pltpu.VMEM((tm, W_C), F32),
            pltpu.VMEM((tm, W_C), F32),
            pltpu.VMEM((nb, W_C), F32),
            pltpu.VMEM((tm, D_MODEL), BF16),
            pltpu.VMEM((tm, D_MODEL), BF16),
            pltpu.VMEM((tm, D_MODEL), BF16),
        ],
        compiler_params=pltpu.CompilerParams(
            dimension_semantics=("arbitrary",), vmem_limit_bytes=VMEM_LIMIT_BYTES),
        name="mixer",
    )(x, x, init_a, init_b, init_c, init_h, *w)


def _block_diag(w):
    half = LRU_HEADS // 2
    eye = jnp.eye(half, dtype=w.dtype)
    w = w.reshape(DEPTH, 2, half, LRU_HD, LRU_HD)
    full = jnp.einsum('lshij,hg->lshigj', w, eye)
    return full.reshape(DEPTH, 2, half * LRU_HD, half * LRU_HD).astype(BF16)


def _mix_group(x, st_a, st_b, st_c, st_h, layer, w, *, nb, tm):
    x, oa, ob, oc, oh = _mixer_call(
        x, st_a.reshape(-1, W_A), st_b.reshape(-1, W_B), st_c.reshape(-1, W_C), st_h,
        layer, w, nb=nb, tm=tm)
    return x, (oa.reshape(K_A - 1, nb, W_A), ob.reshape(K_B - 1, nb, W_B), oc.reshape(K_C - 1, nb, W_C), oh)


def kernel(x_prompt, x_sample, state_conv_a, state_conv_b, state_conv_c, state_lru_h, norm_ffn1, w1_up, w1_down, norm_mix, w_in, conv_a_w, conv_b_w, conv_b_b, ln_b_g, ln_b_b, conv_c_w, conv_c_b, lru_wa, lru_ba, lru_wx, lru_bx, lru_lam, grp_g, w_out, norm_ffn2, w2_up, w2_down, final_norm):
    bp, seq, _ = x_prompt.shape
    bs = x_sample.shape[0]

    row = lambda v: v[:, None, :]
    p = {
        'g1': row(norm_ffn1), 'wup1': w1_up, 'wd1': w1_down,
        'g2': row(norm_ffn2), 'wup2': w2_up, 'wd2': w2_down,
        'gf': final_norm[None, :],
        'mix': (row(norm_mix), w_in.astype(BF16), conv_a_w, conv_b_w, row(conv_b_b), row(ln_b_g), row(ln_b_b),
                conv_c_w, row(conv_c_b), _block_diag(lru_wa), row(lru_ba), _block_diag(lru_wx), row(lru_bx),
                row(lru_lam), row(grp_g), w_out.astype(BF16)),
    }

    zeros = lambda k, c: jnp.zeros((k, bp, c), F32)
    tmaj = lambda s: s.transpose(0, 2, 1, 3)
    st_a, st_b, st_c = tmaj(state_conv_a), tmaj(state_conv_b), tmaj(state_conv_c)
    xp, xs = x_prompt, x_sample.reshape(bs, D_MODEL)
    p_states, s_states = [], []
    for l in range(DEPTH):
        xp, xs = _ffn_call(xp, xs, l, p['g1'], p['wup1'], p['wd1'], p['gf'], tm=TM_PROMPT, final=False,
                           layout='from_seq' if l == 0 else 'rows', nb=bp)
        xp, st = _mix_group(xp, zeros(K_A - 1, W_A), zeros(K_B - 1, W_B), zeros(K_C - 1, W_C),
                            jnp.zeros((bp, W_C), F32), l, p['mix'], nb=bp, tm=TM_PROMPT)
        p_states.append(st)
        xs, st = _mix_group(xs, st_a[l], st_b[l], st_c[l], state_lru_h[l], l, p['mix'], nb=bs, tm=bs)
        s_states.append(st)
        xp, xs = _ffn_call(xp, xs, l, p['g2'], p['wup2'], p['wd2'], p['gf'], tm=TM_PROMPT,
                           final=(l == DEPTH - 1), layout='to_seq' if l == DEPTH - 1 else 'rows', nb=bp)
    y_prompt, ys = xp, xs

    def stack_states(states):
        a, b, c, h = (jnp.stack(s) for s in zip(*states))
        return tmaj(a), tmaj(b), tmaj(c), h

    pa, pb, pc, ph = stack_states(p_states)
    sa, sb, sc, sh = stack_states(s_states)
    y_sample = ys.reshape(bs, 1, D_MODEL)
    return (y_prompt, y_sample, pa, pb, pc, ph, sa, sb, sc, sh)
```

```python
import functools

import jax
import jax.numpy as jnp
from jax import lax
from jax.experimental import pallas as pl
from jax.experimental.pallas import tpu as pltpu

D_MODEL = 1024
DEPTH = 4
W_A = 256
W_B = 256
W_C = 512
K_A = 3
K_B = 31
K_C = 4
LRU_HEADS = 8
LRU_HD = W_C // LRU_HEADS
LRU_C = 8.0
D_FF = 2816
RMS_EPS = 1e-6
LN_EPS = 1e-5
IN_WIDTH = 3 * W_A + 2 * W_B + 2 * W_C

FF_CHUNK = 256
N_FF_CHUNKS = D_FF // FF_CHUNK
MM_CHUNK = 256
TM_PROMPT = 512
TM_FFN = 1024
SUBLANES = 8
VMEM_LIMIT_BYTES = 62 * 1024 * 1024

F32 = jnp.float32
BF16 = jnp.bfloat16


def _rms(x, g):
    return x * lax.rsqrt(jnp.mean(x * x, axis=-1, keepdims=True) + RMS_EPS) * g


def _sigmoid(x):
    return 1.0 / (1.0 + jnp.exp(-x))


def _const_spec(shape, layer):
    nd = len(shape)
    return pl.BlockSpec((None,) + tuple(shape), lambda i: (layer,) + (0,) * nd,
                        pipeline_mode=pl.Buffered(1))


def _swiglu_half_step(x, g_ref, wup_ref, wd_ref, gf_ref, final):
    h = _rms(x, g_ref[...]).astype(BF16)
    acc = jnp.zeros(x.shape, F32)
    for c in range(N_FF_CHUNKS):
        lo = c * FF_CHUNK
        wg = wup_ref[:, lo:lo + FF_CHUNK].astype(BF16)
        wu = wup_ref[:, D_FF + lo:D_FF + lo + FF_CHUNK].astype(BF16)
        wd = wd_ref[lo:lo + FF_CHUNK, :].astype(BF16)
        g = jnp.dot(h, wg, preferred_element_type=F32)
        u = jnp.dot(h, wu, preferred_element_type=F32)
        a = (g * _sigmoid(g) * u).astype(BF16)
        acc = acc + jnp.dot(a, wd, preferred_element_type=F32)
    y = x + 0.5 * acc
    if final:
        y = _rms(y, gf_ref[...])
    return y


def _ffn_kernel(x_ref, xs_ref, g_ref, wup_ref, wd_ref, gf_ref, o_ref, os_ref, *, final, layout, nb):
    if layout == 'from_seq':
        tq = x_ref.shape[1]
        x = x_ref[...].reshape(nb * tq, D_MODEL)
    elif layout == 'to_seq':
        tq = x_ref.shape[0] // nb
        x = jnp.swapaxes(x_ref[...].reshape(tq, nb, D_MODEL), 0, 1).reshape(nb * tq, D_MODEL)
    else:
        x = x_ref[...]
    y = _swiglu_half_step(x, g_ref, wup_ref, wd_ref, gf_ref, final)
    if layout == 'from_seq':
        o_ref[...] = jnp.swapaxes(y.reshape(nb, tq, D_MODEL), 0, 1).reshape(tq * nb, D_MODEL)
    elif layout == 'to_seq':
        o_ref[...] = y.reshape(nb, tq, D_MODEL)
    else:
        o_ref[...] = y

    @pl.when(pl.program_id(0) == pl.num_programs(0) - 1)
    def _():
        os_ref[...] = _swiglu_half_step(xs_ref[...], g_ref, wup_ref, wd_ref, gf_ref, final)


def _ffn_call(x, xs, layer, norm_g, wup, wd, final_g, *, tm, final, layout='rows', nb=1):
    m = x.shape[0] * x.shape[1] if layout == 'from_seq' else x.shape[0]
    ms = xs.shape[0]
    rows_spec = pl.BlockSpec((tm, D_MODEL), lambda i: (i, 0))
    seq_spec = pl.BlockSpec((nb, tm // nb, D_MODEL), lambda i: (0, i, 0))
    sample_spec = pl.BlockSpec((ms, D_MODEL), lambda i: (0, 0))
    return pl.pallas_call(
        functools.partial(_ffn_kernel, final=final, layout=layout, nb=nb),
        grid=(m // tm,),
        in_specs=[
            seq_spec if layout == 'from_seq' else rows_spec,
            sample_spec,
            _const_spec((1, D_MODEL), layer),
            _const_spec((D_MODEL, 2 * D_FF), layer),
            _const_spec((D_FF, D_MODEL), layer),
            pl.BlockSpec((1, D_MODEL), lambda i: (0, 0)),
        ],
        out_specs=[seq_spec if layout == 'to_seq' else rows_spec, sample_spec],
        out_shape=[
            jax.ShapeDtypeStruct((nb, m // nb, D_MODEL) if layout == 'to_seq' else (m, D_MODEL), F32),
            jax.ShapeDtypeStruct((ms, D_MODEL), F32),
        ],
        compiler_params=pltpu.CompilerParams(
            dimension_semantics=("arbitrary",), vmem_limit_bytes=VMEM_LIMIT_BYTES),
        name="ffn",
    )(x, xs, norm_g, wup, wd, final_g)


def _mixer_kernel(x_ref, xr_ref, ia_ref, ib_ref, ic_ref, ih_ref,
                  gmix_ref, win_ref, caw_ref, cbw_ref, cbb_ref, lng_ref, lnb_ref,
                  ccw_ref, ccb_ref, wa_ref, ba_ref, wx_ref, bx_ref, lam_ref, gg_ref, wout_ref,
                  o_ref, oa_ref, ob_ref, oc_ref, oh_ref,
                  u0_ref, u1_ref, pa_ref, gb_ref, cx_ref, xc_ref, xcb_ref, g_ref, a_ref, b_ref, h_ref,
                  y0_ref, y1_ref, hx_ref, *, nb, tm, rb, gm, n_tiles):
    i = pl.program_id(0)
    ha, hb, hc = (K_A - 1) * nb, (K_B - 1) * nb, (K_C - 1) * nb
    pipelined = n_tiles > 1

    @pl.when(i == 0)
    def _():
        pa_ref[0:ha, :] = ia_ref[...]
        gb_ref[0:hb, :] = ib_ref[...]
        cx_ref[0:hc, :] = ic_ref[...]
        h_ref[...] = ih_ref[...]
        if pipelined:
            u1_ref[...] = jnp.zeros((tm, IN_WIDTH), F32)
            y0_ref[...] = jnp.zeros((tm, D_MODEL), BF16)
            y1_ref[...] = jnp.zeros((tm, D_MODEL), BF16)

    step = functools.partial(
        _mixer_step, x_ref, xr_ref, gmix_ref, win_ref, caw_ref, cbw_ref, cbb_ref, lng_ref, lnb_ref,
        ccw_ref, ccb_ref, wa_ref, ba_ref, wx_ref, bx_ref, lam_ref, gg_ref, wout_ref, o_ref,
        pa_ref, gb_ref, cx_ref, xc_ref, xcb_ref, g_ref, a_ref, b_ref, h_ref, hx_ref,
        nb=nb, tm=tm, rb=rb, gm=gm, n_tiles=n_tiles, pipelined=pipelined)

    if pipelined:
        @pl.when(i % 2 == 0)
        def _():
            step(u0_ref, u1_ref, y0_ref, y1_ref)

        @pl.when(i % 2 == 1)
        def _():
            step(u1_ref, u0_ref, y1_ref, y0_ref)
    else:
        step(u0_ref, u0_ref, y0_ref, y0_ref)

    @pl.when(i == pl.num_programs(0) - 1)
    def _():
        oa_ref[...] = pa_ref[0:ha, :]
        ob_ref[...] = gb_ref[0:hb, :]
        oc_ref[...] = cx_ref[0:hc, :]
        oh_ref[...] = h_ref[...]


def _mixer_step(x_ref, xr_ref, gmix_ref, win_ref, caw_ref, cbw_ref, cbb_ref, lng_ref, lnb_ref,
                ccw_ref, ccb_ref, wa_ref, ba_ref, wx_ref, bx_ref, lam_ref, gg_ref, wout_ref, o_ref,
                pa_ref, gb_ref, cx_ref, xc_ref, xcb_ref, g_ref, a_ref, b_ref, h_ref, hx_ref,
                u_new, us, y_old, ys, *, nb, tm, rb, gm, n_tiles, pipelined):
    i = pl.program_id(0)
    ha, hb, hc = (K_A - 1) * nb, (K_B - 1) * nb, (K_C - 1) * nb
    n_rb = tm // rb
    tt = tm // nb

    hx_ref[...] = _rms(x_ref[...], gmix_ref[...]).astype(BF16)

    def in_proj(c):
        cols = slice(c * MM_CHUNK, (c + 1) * MM_CHUNK)
        u_new[:, cols] = jnp.dot(hx_ref[...], win_ref[:, cols], preferred_element_type=F32)

    def out_proj(c):
        cols = slice(c * MM_CHUNK, (c + 1) * MM_CHUNK)
        o_ref[:, cols] = xr_ref[:, cols] + jnp.dot(y_old[...], wout_ref[:, cols],
                                                   preferred_element_type=F32)

    sp = lam_ref[...]
    sp = jnp.maximum(-sp, 0.0) + jnp.log1p(jnp.exp(-jnp.abs(sp)))

    def convs(r):
        r0 = r * rb
        rows = slice(r0, r0 + rb)
        pa_ref[ha + r0:ha + r0 + rb, :] = us[rows, 256:512] * us[rows, 512:768]
        gb_ref[hb + r0:hb + r0 + rb, :] = us[rows, 768:1024] * _sigmoid(us[rows, 1024:1280])
        cx_ref[hc + r0:hc + r0 + rb, :] = us[rows, 1792:2304]
        ca = pa_ref[r0:r0 + rb, :] * caw_ref[0:1, :]
        for k in range(1, K_A):
            ca = ca + pa_ref[r0 + k * nb:r0 + k * nb + rb, :] * caw_ref[k:k + 1, :]
        ya = us[rows, 0:256] * ca
        ys[rows, 0:256] = _rms(ya, gg_ref[:, 0:256]).astype(BF16)
        cb = gb_ref[r0:r0 + rb, :] * cbw_ref[0:1, :]
        for k in range(1, K_B):
            cb = cb + gb_ref[r0 + k * nb:r0 + k * nb + rb, :] * cbw_ref[k:k + 1, :]
        cb = cb + cbb_ref[...]
        mu = jnp.mean(cb, axis=-1, keepdims=True)
        cc = cb - mu
        var = jnp.mean(cc * cc, axis=-1, keepdims=True)
        ln = cc * lax.rsqrt(var + LN_EPS) * lng_ref[...] + lnb_ref[...]
        yb = ln * _sigmoid(ln)
        ys[rows, 256:512] = _rms(yb, gg_ref[:, 256:512]).astype(BF16)
        xc = cx_ref[r0:r0 + rb, :] * ccw_ref[0:1, :]
        for k in range(1, K_C):
            xc = xc + cx_ref[r0 + k * nb:r0 + k * nb + rb, :] * ccw_ref[k:k + 1, :]
        xc = xc + ccb_ref[...]
        xc_ref[rows, :] = xc
        xcb_ref[rows, :] = xc.astype(BF16)
        if (r0 + rb) % gm == 0:
            grows = slice(r0 + rb - gm, r0 + rb)
            for half in range(2):
                cols = slice(half * 256, half * 256 + 256)
                g_ref[grows, cols] = jnp.dot(xcb_ref[grows, cols], wa_ref[half],
                                             preferred_element_type=F32)
                g_ref[grows, 512 + half * 256:768 + half * 256] = jnp.dot(
                    xcb_ref[grows, cols], wx_ref[half], preferred_element_type=F32)

    def coeffs(r):
        rows = slice(r * rb, r * rb + rb)
        rg = _sigmoid(g_ref[rows, 0:512] + ba_ref[...])
        ig = _sigmoid(g_ref[rows, 512:1024] + bx_ref[...])
        log_a = (-LRU_C) * rg * sp
        a = jnp.exp(log_a)
        v = jnp.maximum(jnp.tanh(-log_a) * (1.0 + a * a), 0.0)
        gain = jnp.where(v > 0.0, v * lax.rsqrt(v), v)
        a_ref[rows, :] = a
        b_ref[rows, :] = gain * (ig * xc_ref[rows, :])

    def gate(r):
        rows = slice(r * rb, r * rb + rb)
        cg = us[rows, 1280:1792]
        gelu = 0.5 * cg * (1.0 + jnp.tanh(0.7978845608028654 * (cg + 0.044715 * (cg * cg * cg))))
        yc = gelu * b_ref[rows, :]
        ys[rows, 512:1024] = _rms(yc, gg_ref[:, 512:1024]).astype(BF16)

    mm_units = ([functools.partial(in_proj, c) for c in range(IN_WIDTH // MM_CHUNK)]
                + [functools.partial(out_proj, c) for c in range(D_MODEL // MM_CHUNK)])
    ew_units = ([(functools.partial(convs, r), 5.0) for r in range(n_rb)]
                + [(functools.partial(coeffs, r), 1.5) for r in range(n_rb)])
    ew_total = sum(cost for _, cost in ew_units) + 1.0 * n_rb
    n_mm_done, ew_done = 0, 0.0

    n_in_proj = IN_WIDTH // MM_CHUNK

    def emit_mm(upto):
        nonlocal n_mm_done
        if pipelined:
            target = min(len(mm_units), int(upto * len(mm_units) + 0.999))
        else:
            target = n_in_proj if upto < 1.0 else len(mm_units)
        while n_mm_done < target:
            mm_units[n_mm_done]()
            n_mm_done += 1

    emit_mm(1.0 / len(mm_units))
    for unit, cost in ew_units:
        unit()
        ew_done += cost
        emit_mm(ew_done / ew_total)

    h = h_ref[...]
    for t in range(tt):
        rows = slice(t * nb, t * nb + nb)
        h = a_ref[rows, :] * h + b_ref[rows, :]
        b_ref[rows, :] = h

    for r in range(n_rb):
        gate(r)
        ew_done += 1.0
        emit_mm(ew_done / ew_total)
    emit_mm(1.0)

    @pl.when(jnp.logical_and(i >= 1, i <= n_tiles) if pipelined else i == 0)
    def _():
        ta = pa_ref[tm:tm + ha, :]
        tb = gb_ref[tm:tm + hb, :]
        tc = cx_ref[tm:tm + hc, :]
        pa_ref[0:ha, :] = ta
        gb_ref[0:hb, :] = tb
        cx_ref[0:hc, :] = tc
        h_ref[...] = h


def _mixer_call(x, init_a, init_b, init_c, init_h, st_layer, layer, w, *, nb, tm):
    m = x.shape[0]
    n_tiles = m // tm
    ha, hb, hc = (K_A - 1) * nb, (K_B - 1) * nb, (K_C - 1) * nb
    rb = max(nb, 64)
    gm = min(tm, 256)
    full = lambda shape: pl.BlockSpec(shape, lambda i: (0,) * len(shape))
    init = lambda shape: pl.BlockSpec((None,) + shape, lambda i: (st_layer, 0, 0))
    tile_new = pl.BlockSpec((tm, D_MODEL), lambda i: (jnp.minimum(i, n_tiles - 1), 0))
    tile_old = pl.BlockSpec((tm, D_MODEL), lambda i: (jnp.clip(i - 2, 0, n_tiles - 1), 0))
    return pl.pallas_call(
        functools.partial(_mixer_kernel, nb=nb, tm=tm, rb=rb, gm=gm, n_tiles=n_tiles),
        grid=(n_tiles + 2 if n_tiles > 1 else 1,),
        in_specs=[
            tile_new, tile_old,
            init((ha, W_A)), init((hb, W_B)), init((hc, W_C)), init((nb, W_C)),
            _const_spec((1, D_MODEL), layer),
            _const_spec((D_MODEL, IN_WIDTH), layer),
            _const_spec((K_A, W_A), layer),
            _const_spec((K_B, W_B), layer),
            _const_spec((1, W_B), layer),
            _const_spec((1, W_B), layer),
            _const_spec((1, W_B), layer),
            _const_spec((K_C, W_C), layer),
            _const_spec((1, W_C), layer),
            _const_spec((2, 256, 256), layer),
            _const_spec((1, W_C), layer),
            _const_spec((2, 256, 256), layer),
            _const_spec((1, W_C), layer),
            _const_spec((1, W_C), layer),
            _const_spec((1, D_MODEL), layer),
            _const_spec((D_MODEL, D_MODEL), layer),
        ],
        out_specs=[
            tile_old,
            full((ha, W_A)), full((hb, W_B)), full((hc, W_C)), full((nb, W_C)),
        ],
        out_shape=[
            jax.ShapeDtypeStruct((m, D_MODEL), F32),
            jax.ShapeDtypeStruct((ha, W_A), F32),
            jax.ShapeDtypeStruct((hb, W_B), F32),
            jax.ShapeDtypeStruct((hc, W_C), F32),
            jax.ShapeDtypeStruct((nb, W_C), F32),
        ],
        scratch_shapes=[
            pltpu.VMEM((tm, IN_WIDTH), F32),
            pltpu.VMEM((tm, IN_WIDTH), F32),
            pltpu.VMEM((ha + tm, W_A), F32),
            pltpu.VMEM((hb + tm, W_B), F32),
            pltpu.VMEM((hc + tm, W_C), F32),
            pltpu.VMEM((tm, W_C), F32),
            pltpu.VMEM((tm, W_C), BF16),
            pltpu.VMEM((tm, 2 * W_C), F32),
            pltpu.VMEM((tm, W_C), F32),
            pltpu.VMEM((tm, W_C), F32),
            pltpu.VMEM((nb, W_C), F32),
            pltpu.VMEM((tm, D_MODEL), BF16),
            pltpu.VMEM((tm, D_MODEL), BF16),
            pltpu.VMEM((tm, D_MODEL), BF16),
        ],
        compiler_params=pltpu.CompilerParams(
            dimension_semantics=("arbitrary",), vmem_limit_bytes=VMEM_LIMIT_BYTES),
        name="mixer",
    )(x, x, init_a, init_b, init_c, init_h, *w)


def _block_diag(w):
    half = LRU_HEADS // 2
    eye = jnp.eye(half, dtype=w.dtype)
    w = w.reshape(DEPTH, 2, half, LRU_HD, LRU_HD)
    full = jnp.einsum('lshij,hg->lshigj', w, eye)
    return full.reshape(DEPTH, 2, half * LRU_HD, half * LRU_HD).astype(BF16)


def _mix_group(x, st, st_layer, layer, w, *, nb, tm):
    x, oa, ob, oc, oh = _mixer_call(x, *st, st_layer, layer, w, nb=nb, tm=tm)
    return x, (oa.reshape(K_A - 1, nb, W_A), ob.reshape(K_B - 1, nb, W_B), oc.reshape(K_C - 1, nb, W_C), oh)


def kernel(x_prompt, x_sample, state_conv_a, state_conv_b, state_conv_c, state_lru_h, norm_ffn1, w1_up, w1_down, norm_mix, w_in, conv_a_w, conv_b_w, conv_b_b, ln_b_g, ln_b_b, conv_c_w, conv_c_b, lru_wa, lru_ba, lru_wx, lru_bx, lru_lam, grp_g, w_out, norm_ffn2, w2_up, w2_down, final_norm):
    bp, seq, _ = x_prompt.shape
    bs = x_sample.shape[0]

    row = lambda v: v[:, None, :]
    p = {
        'g1': row(norm_ffn1), 'wup1': w1_up, 'wd1': w1_down,
        'g2': row(norm_ffn2), 'wup2': w2_up, 'wd2': w2_down,
        'gf': final_norm[None, :],
        'mix': (row(norm_mix), w_in.astype(BF16), conv_a_w, conv_b_w, row(conv_b_b), row(ln_b_g), row(ln_b_b),
                conv_c_w, row(conv_c_b), _block_diag(lru_wa), row(lru_ba), _block_diag(lru_wx), row(lru_bx),
                row(lru_lam), row(grp_g), w_out.astype(BF16)),
    }

    tmaj = lambda s: s.transpose(0, 2, 1, 3)
    rows = lambda s: s.reshape(s.shape[0], -1, s.shape[-1])
    p_init = (jnp.zeros((1, (K_A - 1) * bp, W_A), F32), jnp.zeros((1, (K_B - 1) * bp, W_B), F32),
              jnp.zeros((1, (K_C - 1) * bp, W_C), F32), jnp.zeros((1, bp, W_C), F32))
    s_init = (rows(tmaj(state_conv_a)), rows(tmaj(state_conv_b)), rows(tmaj(state_conv_c)), state_lru_h)
    xp, xs = x_prompt, x_sample.reshape(bs, D_MODEL)
    p_states, s_states = [], []
    for l in range(DEPTH):
        xp, xs = _ffn_call(xp, xs, l, p['g1'], p['wup1'], p['wd1'], p['gf'], tm=TM_FFN, final=False,
                           layout='from_seq' if l == 0 else 'rows', nb=bp)
        xp, st = _mix_group(xp, p_init, 0, l, p['mix'], nb=bp, tm=TM_PROMPT)
        p_states.append(st)
        xs, st = _mix_group(xs, s_init, l, l, p['mix'], nb=bs, tm=bs)
        s_states.append(st)
        xp, xs = _ffn_call(xp, xs, l, p['g2'], p['wup2'], p['wd2'], p['gf'], tm=TM_FFN,
                           final=(l == DEPTH - 1), layout='to_seq' if l == DEPTH - 1 else 'rows', nb=bp)
    y_prompt, ys = xp, xs

    def stack_states(states):
        a, b, c, h = (jnp.stack(s) for s in zip(*states))
        return tmaj(a), tmaj(b), tmaj(c), h

    pa, pb, pc, ph = stack_states(p_states)
    sa, sb, sc, sh = stack_states(s_states)
    y_sample = ys.reshape(bs, 1, D_MODEL)
    return (y_prompt, y_sample, pa, pb, pc, ph, sa, sb, sc, sh)
```

```python
import functools

import jax
import jax.numpy as jnp
from jax import lax
from jax.experimental import pallas as pl
from jax.experimental.pallas import tpu as pltpu

D_MODEL = 1024
DEPTH = 4
W_A = 256
W_B = 256
W_C = 512
K_A = 3
K_B = 31
K_C = 4
LRU_HEADS = 8
LRU_HD = W_C // LRU_HEADS
LRU_C = 8.0
D_FF = 2816
RMS_EPS = 1e-6
LN_EPS = 1e-5
IN_WIDTH = 3 * W_A + 2 * W_B + 2 * W_C

FF_CHUNK = 256
N_FF_CHUNKS = D_FF // FF_CHUNK
MM_CHUNK = 256
TM_PROMPT = 512
TM_FFN = 1024
SUBLANES = 8
VMEM_LIMIT_BYTES = 62 * 1024 * 1024

F32 = jnp.float32
BF16 = jnp.bfloat16


def _rms(x, g):
    return x * lax.rsqrt(jnp.mean(x * x, axis=-1, keepdims=True) + RMS_EPS) * g


def _sigmoid(x):
    return 1.0 / (1.0 + jnp.exp(-x))


def _const_spec(shape, layer):
    nd = len(shape)
    return pl.BlockSpec((None,) + tuple(shape), lambda i: (layer,) + (0,) * nd,
                        pipeline_mode=pl.Buffered(1))


def _swiglu_half_step(x, g_ref, wup_ref, wd_ref, gf_ref, final):
    h = _rms(x, g_ref[...]).astype(BF16)
    acc = jnp.zeros(x.shape, F32)
    for c in range(N_FF_CHUNKS):
        lo = c * FF_CHUNK
        wg = wup_ref[:, lo:lo + FF_CHUNK].astype(BF16)
        wu = wup_ref[:, D_FF + lo:D_FF + lo + FF_CHUNK].astype(BF16)
        wd = wd_ref[lo:lo + FF_CHUNK, :].astype(BF16)
        g = jnp.dot(h, wg, preferred_element_type=F32)
        u = jnp.dot(h, wu, preferred_element_type=F32)
        a = (g * _sigmoid(g) * u).astype(BF16)
        acc = acc + jnp.dot(a, wd, preferred_element_type=F32)
    y = x + 0.5 * acc
    if final:
        y = _rms(y, gf_ref[...])
    return y


def _ffn_kernel(x_ref, xs_ref, g_ref, wup_ref, wd_ref, gf_ref, o_ref, os_ref, *, final, layout, nb):
    if layout == 'from_seq':
        tq = x_ref.shape[1]
        x = x_ref[...].reshape(nb * tq, D_MODEL)
    else:
        x = x_ref[...]
    y = _swiglu_half_step(x, g_ref, wup_ref, wd_ref, gf_ref, final)
    if layout == 'from_seq':
        o_ref[...] = jnp.swapaxes(y.reshape(nb, tq, D_MODEL), 0, 1).reshape(tq * nb, D_MODEL)
    elif layout == 'to_seq':
        o_ref[...] = jnp.swapaxes(y.reshape(y.shape[0] // nb, nb, D_MODEL), 0, 1)
    else:
        o_ref[...] = y

    @pl.when(pl.program_id(0) == pl.num_programs(0) - 1)
    def _():
        os_ref[...] = _swiglu_half_step(xs_ref[...], g_ref, wup_ref, wd_ref, gf_ref, final)


def _ffn_call(x, xs, layer, norm_g, wup, wd, final_g, *, tm, final, layout='rows', nb=1):
    m = x.shape[0] * x.shape[1] if layout == 'from_seq' else x.shape[0]
    ms = xs.shape[0]
    rows_spec = pl.BlockSpec((tm, D_MODEL), lambda i: (i, 0))
    seq_spec = pl.BlockSpec((nb, tm // nb, D_MODEL), lambda i: (0, i, 0))
    sample_spec = pl.BlockSpec((ms, D_MODEL), lambda i: (0, 0))
    return pl.pallas_call(
        functools.partial(_ffn_kernel, final=final, layout=layout, nb=nb),
        grid=(m // tm,),
        in_specs=[
            seq_spec if layout == 'from_seq' else rows_spec,
            sample_spec,
            _const_spec((1, D_MODEL), layer),
            _const_spec((D_MODEL, 2 * D_FF), layer),
            _const_spec((D_FF, D_MODEL), layer),
            pl.BlockSpec((1, D_MODEL), lambda i: (0, 0)),
        ],
        out_specs=[seq_spec if layout == 'to_seq' else rows_spec, sample_spec],
        out_shape=[
            jax.ShapeDtypeStruct((nb, m // nb, D_MODEL) if layout == 'to_seq' else (m, D_MODEL), F32),
            jax.ShapeDtypeStruct((ms, D_MODEL), F32),
        ],
        compiler_params=pltpu.CompilerParams(
            dimension_semantics=("arbitrary",), vmem_limit_bytes=VMEM_LIMIT_BYTES),
        name="ffn",
    )(x, xs, norm_g, wup, wd, final_g)


def _mixer_kernel(x_ref, xr_ref, ia_ref, ib_ref, ic_ref, ih_ref,
                  gmix_ref, win_ref, caw_ref, cbw_ref, cbb_ref, lng_ref, lnb_ref,
                  ccw_ref, ccb_ref, wa_ref, ba_ref, wx_ref, bx_ref, lam_ref, gg_ref, wout_ref,
                  o_ref, oa_ref, ob_ref, oc_ref, oh_ref,
                  u0_ref, u1_ref, pa_ref, gb_ref, cx_ref, xc_ref, xcb_ref, g_ref, a_ref, b_ref, h_ref,
                  y0_ref, y1_ref, hx_ref, *, nb, tm, rb, gm, n_tiles):
    i = pl.program_id(0)
    ha, hb, hc = (K_A - 1) * nb, (K_B - 1) * nb, (K_C - 1) * nb
    pipelined = n_tiles > 1

    @pl.when(i == 0)
    def _():
        pa_ref[0:ha, :] = ia_ref[...]
        gb_ref[0:hb, :] = ib_ref[...]
        cx_ref[0:hc, :] = ic_ref[...]
        h_ref[...] = ih_ref[...]
        if pipelined:
            y1_ref[...] = jnp.zeros((tm, D_MODEL), BF16)

    step = functools.partial(
        _mixer_step, x_ref, xr_ref, gmix_ref, win_ref, caw_ref, cbw_ref, cbb_ref, lng_ref, lnb_ref,
        ccw_ref, ccb_ref, wa_ref, ba_ref, wx_ref, bx_ref, lam_ref, gg_ref, wout_ref, o_ref,
        pa_ref, gb_ref, cx_ref, xc_ref, xcb_ref, g_ref, a_ref, b_ref, h_ref, hx_ref,
        nb=nb, tm=tm, rb=rb, gm=gm)

    if pipelined:
        slots = ((u0_ref, u1_ref, y0_ref, y1_ref), (u1_ref, u0_ref, y1_ref, y0_ref))
        last = n_tiles + 1
        busy = jnp.logical_and(i >= 1, i <= n_tiles)

        @pl.when(i == 0)
        def _():
            step(*slots[0], stages='in')

        @pl.when(jnp.logical_and(busy, i % 2 == 0))
        def _():
            step(*slots[0], stages='all')

        @pl.when(jnp.logical_and(busy, i % 2 == 1))
        def _():
            step(*slots[1], stages='all')

        @pl.when(i == last)
        def _():
            step(*slots[last % 2], stages='out')
    else:
        step(u0_ref, u0_ref, y0_ref, y0_ref, stages='in_order')

    @pl.when(i == pl.num_programs(0) - 1)
    def _():
        oa_ref[...] = pa_ref[0:ha, :]
        ob_ref[...] = gb_ref[0:hb, :]
        oc_ref[...] = cx_ref[0:hc, :]
        oh_ref[...] = h_ref[...]


def _mixer_step(x_ref, xr_ref, gmix_ref, win_ref, caw_ref, cbw_ref, cbb_ref, lng_ref, lnb_ref,
                ccw_ref, ccb_ref, wa_ref, ba_ref, wx_ref, bx_ref, lam_ref, gg_ref, wout_ref, o_ref,
                pa_ref, gb_ref, cx_ref, xc_ref, xcb_ref, g_ref, a_ref, b_ref, h_ref, hx_ref,
                u_new, us, y_old, ys, *, nb, tm, rb, gm, stages):
    ha, hb, hc = (K_A - 1) * nb, (K_B - 1) * nb, (K_C - 1) * nb
    n_rb = tm // rb
    tt = tm // nb
    n_in_proj, n_out_proj = IN_WIDTH // MM_CHUNK, D_MODEL // MM_CHUNK
    pipelined = stages == 'all'

    def in_proj(c):
        cols = slice(c * MM_CHUNK, (c + 1) * MM_CHUNK)
        u_new[:, cols] = jnp.dot(hx_ref[...], win_ref[:, cols], preferred_element_type=F32)

    def out_proj(c):
        cols = slice(c * MM_CHUNK, (c + 1) * MM_CHUNK)
        o_ref[:, cols] = xr_ref[:, cols] + jnp.dot(y_old[...], wout_ref[:, cols],
                                                   preferred_element_type=F32)

    if stages == 'out':
        for c in range(n_out_proj):
            out_proj(c)
        return

    hx_ref[...] = _rms(x_ref[...], gmix_ref[...]).astype(BF16)
    if stages == 'in':
        for c in range(n_in_proj):
            in_proj(c)
        return


    sp = lam_ref[...]
    sp = jnp.maximum(-sp, 0.0) + jnp.log1p(jnp.exp(-jnp.abs(sp)))
    neg_c_sp = (-LRU_C) * sp

    def convs(r):
        r0 = r * rb
        rows = slice(r0, r0 + rb)
        pa_ref[ha + r0:ha + r0 + rb, :] = us[rows, 256:512] * us[rows, 512:768]
        gb_ref[hb + r0:hb + r0 + rb, :] = us[rows, 768:1024] * _sigmoid(us[rows, 1024:1280])
        cx_ref[hc + r0:hc + r0 + rb, :] = us[rows, 1792:2304]
        ca = pa_ref[r0:r0 + rb, :] * caw_ref[0:1, :]
        for k in range(1, K_A):
            ca = ca + pa_ref[r0 + k * nb:r0 + k * nb + rb, :] * caw_ref[k:k + 1, :]
        ya = us[rows, 0:256] * ca
        ys[rows, 0:256] = _rms(ya, gg_ref[:, 0:256]).astype(BF16)
        cb = gb_ref[r0:r0 + rb, :] * cbw_ref[0:1, :]
        for k in range(1, K_B):
            cb = cb + gb_ref[r0 + k * nb:r0 + k * nb + rb, :] * cbw_ref[k:k + 1, :]
        cb = cb + cbb_ref[...]
        mu = jnp.mean(cb, axis=-1, keepdims=True)
        cc = cb - mu
        var = jnp.mean(cc * cc, axis=-1, keepdims=True)
        ln = cc * lax.rsqrt(var + LN_EPS) * lng_ref[...] + lnb_ref[...]
        yb = ln * _sigmoid(ln)
        ys[rows, 256:512] = _rms(yb, gg_ref[:, 256:512]).astype(BF16)
        xc = cx_ref[r0:r0 + rb, :] * ccw_ref[0:1, :]
        for k in range(1, K_C):
            xc = xc + cx_ref[r0 + k * nb:r0 + k * nb + rb, :] * ccw_ref[k:k + 1, :]
        xc = xc + ccb_ref[...]
        xc_ref[rows, :] = xc
        xcb_ref[rows, :] = xc.astype(BF16)
        if (r0 + rb) % gm == 0:
            grows = slice(r0 + rb - gm, r0 + rb)
            for half in range(2):
                cols = slice(half * 256, half * 256 + 256)
                g_ref[grows, cols] = jnp.dot(xcb_ref[grows, cols], wa_ref[half],
                                             preferred_element_type=F32)
                g_ref[grows, 512 + half * 256:768 + half * 256] = jnp.dot(
                    xcb_ref[grows, cols], wx_ref[half], preferred_element_type=F32)

    def coeffs(r):
        rows = slice(r * rb, r * rb + rb)
        rg = _sigmoid(g_ref[rows, 0:512] + ba_ref[...])
        ig = _sigmoid(g_ref[rows, 512:1024] + bx_ref[...])
        log_a = rg * neg_c_sp
        a = jnp.exp(log_a)
        v = jnp.maximum(jnp.tanh(-log_a) * (1.0 + a * a), 0.0)
        gain = jnp.where(v > 0.0, v * lax.rsqrt(v), v)
        a_ref[rows, :] = a
        b_ref[rows, :] = gain * (ig * xc_ref[rows, :])

    def gate(r):
        rows = slice(r * rb, r * rb + rb)
        cg = us[rows, 1280:1792]
        gelu = 0.5 * cg * (1.0 + jnp.tanh(0.7978845608028654 * (cg + 0.044715 * (cg * cg * cg))))
        yc = gelu * b_ref[rows, :]
        ys[rows, 512:1024] = _rms(yc, gg_ref[:, 512:1024]).astype(BF16)

    mm_units = ([functools.partial(in_proj, c) for c in range(n_in_proj)]
                + [functools.partial(out_proj, c) for c in range(n_out_proj)])
    ew_units = ([(functools.partial(convs, r), 5.0) for r in range(n_rb)]
                + [(functools.partial(coeffs, r), 1.5) for r in range(n_rb)])
    ew_total = sum(cost for _, cost in ew_units) + 1.0 * n_rb
    n_mm_done, ew_done = 0, 0.0

    def emit_mm(upto):
        nonlocal n_mm_done
        if pipelined:
            target = min(len(mm_units), int(upto * len(mm_units) + 0.999))
        else:
            target = n_in_proj if upto < 1.0 else len(mm_units)
        while n_mm_done < target:
            mm_units[n_mm_done]()
            n_mm_done += 1

    emit_mm(1.0 / len(mm_units))
    for unit, cost in ew_units:
        unit()
        ew_done += cost
        emit_mm(ew_done / ew_total)

    h = h_ref[...]
    for t in range(tt):
        rows = slice(t * nb, t * nb + nb)
        h = a_ref[rows, :] * h + b_ref[rows, :]
        b_ref[rows, :] = h

    for r in range(n_rb):
        gate(r)
        ew_done += 1.0
        emit_mm(ew_done / ew_total)
    emit_mm(1.0)

    ta = pa_ref[tm:tm + ha, :]
    tb = gb_ref[tm:tm + hb, :]
    tc = cx_ref[tm:tm + hc, :]
    pa_ref[0:ha, :] = ta
    gb_ref[0:hb, :] = tb
    cx_ref[0:hc, :] = tc
    h_ref[...] = h


def _mixer_call(x, init_a, init_b, init_c, init_h, st_layer, layer, w, *, nb, tm):
    m = x.shape[0]
    n_tiles = m // tm
    ha, hb, hc = (K_A - 1) * nb, (K_B - 1) * nb, (K_C - 1) * nb
    rb = max(nb, 64)
    gm = min(tm, 256)
    full = lambda shape: pl.BlockSpec(shape, lambda i: (0,) * len(shape))
    init = lambda shape: pl.BlockSpec((None,) + shape, lambda i: (st_layer, 0, 0))
    tile_new = pl.BlockSpec((tm, D_MODEL), lambda i: (jnp.minimum(i, n_tiles - 1), 0))
    tile_old = pl.BlockSpec((tm, D_MODEL), lambda i: (jnp.clip(i - 2, 0, n_tiles - 1), 0))
    return pl.pallas_call(
        functools.partial(_mixer_kernel, nb=nb, tm=tm, rb=rb, gm=gm, n_tiles=n_tiles),
        grid=(n_tiles + 2 if n_tiles > 1 else 1,),
        in_specs=[
            tile_new, tile_old,
            init((ha, W_A)), init((hb, W_B)), init((hc, W_C)), init((nb, W_C)),
            _const_spec((1, D_MODEL), layer),
            _const_spec((D_MODEL, IN_WIDTH), layer),
            _const_spec((K_A, W_A), layer),
            _const_spec((K_B, W_B), layer),
            _const_spec((1, W_B), layer),
            _const_spec((1, W_B), layer),
            _const_spec((1, W_B), layer),
            _const_spec((K_C, W_C), layer),
            _const_spec((1, W_C), layer),
            _const_spec((2, 256, 256), layer),
            _const_spec((1, W_C), layer),
            _const_spec((2, 256, 256), layer),
            _const_spec((1, W_C), layer),
            _const_spec((1, W_C), layer),
            _const_spec((1, D_MODEL), layer),
            _const_spec((D_MODEL, D_MODEL), layer),
        ],
        out_specs=[
            tile_old,
            full((ha, W_A)), full((hb, W_B)), full((hc, W_C)), full((nb, W_C)),
        ],
        out_shape=[
            jax.ShapeDtypeStruct((m, D_MODEL), F32),
            jax.ShapeDtypeStruct((ha, W_A), F32),
            jax.ShapeDtypeStruct((hb, W_B), F32),
            jax.ShapeDtypeStruct((hc, W_C), F32),
            jax.ShapeDtypeStruct((nb, W_C), F32),
        ],
        scratch_shapes=[
            pltpu.VMEM((tm, IN_WIDTH), F32),
            pltpu.VMEM((tm, IN_WIDTH), F32),
            pltpu.VMEM((ha + tm, W_A), F32),
            pltpu.VMEM((hb + tm, W_B), F32),
            pltpu.VMEM((hc + tm, W_C), F32),
            pltpu.VMEM((tm, W_C), F32),
            pltpu.VMEM((tm, W_C), BF16),
            pltpu.VMEM((tm, 2 * W_C), F32),
            pltpu.VMEM((tm, W_C), F32),
            pltpu.VMEM((tm, W_C), F32),
            pltpu.VMEM((nb, W_C), F32),
            pltpu.VMEM((tm, D_MODEL), BF16),
            pltpu.VMEM((tm, D_MODEL), BF16),
            pltpu.VMEM((tm, D_MODEL), BF16),
        ],
        compiler_params=pltpu.CompilerParams(
            dimension_semantics=("arbitrary",), vmem_limit_bytes=VMEM_LIMIT_BYTES),
        name="mixer",
    )(x, x, init_a, init_b, init_c, init_h, *w)


def _block_diag(w):
    half = LRU_HEADS // 2
    eye = jnp.eye(half, dtype=w.dtype)
    w = w.reshape(DEPTH, 2, half, LRU_HD, LRU_HD)
    full = jnp.einsum('lshij,hg->lshigj', w, eye)
    return full.reshape(DEPTH, 2, half * LRU_HD, half * LRU_HD).astype(BF16)


def _mix_group(x, st, st_layer, layer, w, *, nb, tm):
    x, oa, ob, oc, oh = _mixer_call(x, *st, st_layer, layer, w, nb=nb, tm=tm)
    return x, (oa.reshape(K_A - 1, nb, W_A), ob.reshape(K_B - 1, nb, W_B), oc.reshape(K_C - 1, nb, W_C), oh)


def kernel(x_prompt, x_sample, state_conv_a, state_conv_b, state_conv_c, state_lru_h, norm_ffn1, w1_up, w1_down, norm_mix, w_in, conv_a_w, conv_b_w, conv_b_b, ln_b_g, ln_b_b, conv_c_w, conv_c_b, lru_wa, lru_ba, lru_wx, lru_bx, lru_lam, grp_g, w_out, norm_ffn2, w2_up, w2_down, final_norm):
    bp, seq, _ = x_prompt.shape
    bs = x_sample.shape[0]

    row = lambda v: v[:, None, :]
    p = {
        'g1': row(norm_ffn1), 'wup1': w1_up, 'wd1': w1_down,
        'g2': row(norm_ffn2), 'wup2': w2_up, 'wd2': w2_down,
        'gf': final_norm[None, :],
        'mix': (row(norm_mix), w_in.astype(BF16), conv_a_w, conv_b_w, row(conv_b_b), row(ln_b_g), row(ln_b_b),
                conv_c_w, row(conv_c_b), _block_diag(lru_wa), row(lru_ba), _block_diag(lru_wx), row(lru_bx),
                row(lru_lam), row(grp_g), w_out.astype(BF16)),
    }

    tmaj = lambda s: s.transpose(0, 2, 1, 3)
    rows = lambda s: s.reshape(s.shape[0], -1, s.shape[-1])
    p_init = (jnp.zeros((1, (K_A - 1) * bp, W_A), F32), jnp.zeros((1, (K_B - 1) * bp, W_B), F32),
              jnp.zeros((1, (K_C - 1) * bp, W_C), F32), jnp.zeros((1, bp, W_C), F32))
    s_init = (rows(tmaj(state_conv_a)), rows(tmaj(state_conv_b)), rows(tmaj(state_conv_c)), state_lru_h)
    xp, xs = x_prompt, x_sample.reshape(bs, D_MODEL)
    p_states, s_states = [], []
    for l in range(DEPTH):
        xp, xs = _ffn_call(xp, xs, l, p['g1'], p['wup1'], p['wd1'], p['gf'], tm=TM_FFN, final=False,
                           layout='from_seq' if l == 0 else 'rows', nb=bp)
        xp, st = _mix_group(xp, p_init, 0, l, p['mix'], nb=bp, tm=TM_PROMPT)
        p_states.append(st)
        xs, st = _mix_group(xs, s_init, l, l, p['mix'], nb=bs, tm=bs)
        s_states.append(st)
        xp, xs = _ffn_call(xp, xs, l, p['g2'], p['wup2'], p['wd2'], p['gf'], tm=TM_FFN,
                           final=(l == DEPTH - 1), layout='to_seq' if l == DEPTH - 1 else 'rows', nb=bp)
    y_prompt, ys = xp, xs

    def stack_states(states):
        a, b, c, h = (jnp.stack(s) for s in zip(*states))
        return tmaj(a), tmaj(b), tmaj(c), h

    pa, pb, pc, ph = stack_states(p_states)
    sa, sb, sc, sh = stack_states(s_states)
    y_sample = ys.reshape(bs, 1, D_MODEL)
    return (y_prompt, y_sample, pa, pb, pc, ph, sa, sb, sc, sh)
```

```python
import functools

import jax
import jax.numpy as jnp
from jax import lax
from jax.experimental import pallas as pl
from jax.experimental.pallas import tpu as pltpu

D_MODEL = 1024
DEPTH = 4
W_A = 256
W_B = 256
W_C = 512
K_A = 3
K_B = 31
K_C = 4
LRU_HEADS = 8
LRU_HD = W_C // LRU_HEADS
LRU_C = 8.0
D_FF = 2816
RMS_EPS = 1e-6
LN_EPS = 1e-5
IN_WIDTH = 3 * W_A + 2 * W_B + 2 * W_C

FF_CHUNK = 256
N_FF_CHUNKS = D_FF // FF_CHUNK
MM_CHUNK = 256
TM_PROMPT = 512
TM_FFN = 1024
SUBLANES = 8
VMEM_LIMIT_BYTES = 62 * 1024 * 1024

F32 = jnp.float32
BF16 = jnp.bfloat16


def _rms(x, g):
    return x * lax.rsqrt(jnp.mean(x * x, axis=-1, keepdims=True) + RMS_EPS) * g


def _sigmoid(x):
    return 1.0 / (1.0 + jnp.exp(-x))


def _const_spec(shape, layer):
    nd = len(shape)
    return pl.BlockSpec((None,) + tuple(shape), lambda i: (layer,) + (0,) * nd,
                        pipeline_mode=pl.Buffered(1))


def _swiglu_half_step(x, g_ref, wup_ref, wd_ref, gf_ref, final):
    h = _rms(x, g_ref[...]).astype(BF16)
    acc = jnp.zeros(x.shape, F32)
    for c in range(N_FF_CHUNKS):
        lo = c * FF_CHUNK
        wg = wup_ref[:, lo:lo + FF_CHUNK].astype(BF16)
        wu = wup_ref[:, D_FF + lo:D_FF + lo + FF_CHUNK].astype(BF16)
        wd = wd_ref[lo:lo + FF_CHUNK, :].astype(BF16)
        g = jnp.dot(h, wg, preferred_element_type=F32)
        u = jnp.dot(h, wu, preferred_element_type=F32)
        a = (g * _sigmoid(g) * u).astype(BF16)
        acc = acc + jnp.dot(a, wd, preferred_element_type=F32)
    y = x + 0.5 * acc
    if final:
        y = _rms(y, gf_ref[...])
    return y


def _ffn_kernel(x_ref, xs_ref, g_ref, wup_ref, wd_ref, gf_ref, o_ref, os_ref, *, final, layout, nb, layer):
    g_ref = g_ref.at[pl.ds(layer, 1)]
    if layout == 'from_seq':
        tq = x_ref.shape[1]
        x = x_ref[...].reshape(nb * tq, D_MODEL)
    else:
        x = x_ref[...]
    y = _swiglu_half_step(x, g_ref, wup_ref, wd_ref, gf_ref, final)
    if layout == 'from_seq':
        o_ref[...] = jnp.swapaxes(y.reshape(nb, tq, D_MODEL), 0, 1).reshape(tq * nb, D_MODEL)
    elif layout == 'to_seq':
        o_ref[...] = jnp.swapaxes(y.reshape(y.shape[0] // nb, nb, D_MODEL), 0, 1)
    else:
        o_ref[...] = y

    @pl.when(pl.program_id(0) == pl.num_programs(0) - 1)
    def _():
        os_ref[...] = _swiglu_half_step(xs_ref[...], g_ref, wup_ref, wd_ref, gf_ref, final)


def _ffn_call(x, xs, layer, norm_g, wup, wd, final_g, *, tm, final, layout='rows', nb=1):
    m = x.shape[0] * x.shape[1] if layout == 'from_seq' else x.shape[0]
    ms = xs.shape[0]
    rows_spec = pl.BlockSpec((tm, D_MODEL), lambda i: (i, 0))
    seq_spec = pl.BlockSpec((nb, tm // nb, D_MODEL), lambda i: (0, i, 0))
    sample_spec = pl.BlockSpec((ms, D_MODEL), lambda i: (0, 0))
    return pl.pallas_call(
        functools.partial(_ffn_kernel, final=final, layout=layout, nb=nb, layer=layer),
        grid=(m // tm,),
        in_specs=[
            seq_spec if layout == 'from_seq' else rows_spec,
            sample_spec,
            pl.BlockSpec((DEPTH, D_MODEL), lambda i: (0, 0)),
            _const_spec((D_MODEL, 2 * D_FF), layer),
            _const_spec((D_FF, D_MODEL), layer),
            pl.BlockSpec((1, D_MODEL), lambda i: (0, 0)),
        ],
        out_specs=[seq_spec if layout == 'to_seq' else rows_spec, sample_spec],
        out_shape=[
            jax.ShapeDtypeStruct((nb, m // nb, D_MODEL) if layout == 'to_seq' else (m, D_MODEL), F32),
            jax.ShapeDtypeStruct((ms, D_MODEL), F32),
        ],
        compiler_params=pltpu.CompilerParams(
            dimension_semantics=("arbitrary",), vmem_limit_bytes=VMEM_LIMIT_BYTES),
        name="ffn",
    )(x, xs, norm_g, wup, wd, final_g)


def _mixer_kernel(x_ref, xr_ref, ia_ref, ib_ref, ic_ref, ih_ref,
                  gmix_ref, win_ref, caw_ref, cbw_ref, cbb_ref, lng_ref, lnb_ref,
                  ccw_ref, ccb_ref, wa_ref, ba_ref, wx_ref, bx_ref, lam_ref, gg_ref, wout_ref,
                  o_ref, oa_ref, ob_ref, oc_ref, oh_ref,
                  u0_ref, u1_ref, pa_ref, gb_ref, cx_ref, xc_ref, xcb_ref, g_ref, a_ref, b_ref, h_ref,
                  y0_ref, y1_ref, hx_ref, *, nb, tm, rb, gm, n_tiles, layer):
    i = pl.program_id(0)
    gmix_ref, cbb_ref, lng_ref, lnb_ref, ccb_ref, ba_ref, bx_ref, lam_ref, gg_ref = (
        r.at[pl.ds(layer, 1)] for r in (gmix_ref, cbb_ref, lng_ref, lnb_ref, ccb_ref, ba_ref, bx_ref,
                                        lam_ref, gg_ref))
    ha, hb, hc = (K_A - 1) * nb, (K_B - 1) * nb, (K_C - 1) * nb
    pipelined = n_tiles > 1

    @pl.when(i == 0)
    def _():
        pa_ref[0:ha, :] = ia_ref[...]
        gb_ref[0:hb, :] = ib_ref[...]
        cx_ref[0:hc, :] = ic_ref[...]
        h_ref[...] = ih_ref[...]
        if pipelined:
            y1_ref[...] = jnp.zeros((tm, D_MODEL), BF16)

    step = functools.partial(
        _mixer_step, x_ref, xr_ref, gmix_ref, win_ref, caw_ref, cbw_ref, cbb_ref, lng_ref, lnb_ref,
        ccw_ref, ccb_ref, wa_ref, ba_ref, wx_ref, bx_ref, lam_ref, gg_ref, wout_ref, o_ref,
        pa_ref, gb_ref, cx_ref, xc_ref, xcb_ref, g_ref, a_ref, b_ref, h_ref, hx_ref,
        nb=nb, tm=tm, rb=rb, gm=gm)

    if pipelined:
        slots = ((u0_ref, u1_ref, y0_ref, y1_ref), (u1_ref, u0_ref, y1_ref, y0_ref))
        last = n_tiles + 1
        busy = jnp.logical_and(i >= 1, i <= n_tiles)

        @pl.when(i == 0)
        def _():
            step(*slots[0], stages='in')

        @pl.when(jnp.logical_and(busy, i % 2 == 0))
        def _():
            step(*slots[0], stages='all')

        @pl.when(jnp.logical_and(busy, i % 2 == 1))
        def _():
            step(*slots[1], stages='all')

        @pl.when(i == last)
        def _():
            step(*slots[last % 2], stages='out')
    else:
        step(u0_ref, u0_ref, y0_ref, y0_ref, stages='in_order')

    @pl.when(i == pl.num_programs(0) - 1)
    def _():
        oa_ref[...] = pa_ref[0:ha, :]
        ob_ref[...] = gb_ref[0:hb, :]
        oc_ref[...] = cx_ref[0:hc, :]
        oh_ref[...] = h_ref[...]


def _mixer_step(x_ref, xr_ref, gmix_ref, win_ref, caw_ref, cbw_ref, cbb_ref, lng_ref, lnb_ref,
                ccw_ref, ccb_ref, wa_ref, ba_ref, wx_ref, bx_ref, lam_ref, gg_ref, wout_ref, o_ref,
                pa_ref, gb_ref, cx_ref, xc_ref, xcb_ref, g_ref, a_ref, b_ref, h_ref, hx_ref,
                u_new, us, y_old, ys, *, nb, tm, rb, gm, stages):
    ha, hb, hc = (K_A - 1) * nb, (K_B - 1) * nb, (K_C - 1) * nb
    n_rb = tm // rb
    tt = tm // nb
    n_in_proj, n_out_proj = IN_WIDTH // MM_CHUNK, D_MODEL // MM_CHUNK
    pipelined = stages == 'all'

    def in_proj(c):
        cols = slice(c * MM_CHUNK, (c + 1) * MM_CHUNK)
        u_new[:, cols] = jnp.dot(hx_ref[...], win_ref[:, cols], preferred_element_type=F32)

    def out_proj(c):
        cols = slice(c * MM_CHUNK, (c + 1) * MM_CHUNK)
        o_ref[:, cols] = xr_ref[:, cols] + jnp.dot(y_old[...], wout_ref[:, cols],
                                                   preferred_element_type=F32)

    if stages == 'out':
        for c in range(n_out_proj):
            out_proj(c)
        return

    hx_ref[...] = _rms(x_ref[...], gmix_ref[...]).astype(BF16)
    if stages == 'in':
        for c in range(n_in_proj):
            in_proj(c)
        return

    sp = lam_ref[...]
    sp = jnp.maximum(-sp, 0.0) + jnp.log1p(jnp.exp(-jnp.abs(sp)))
    neg_c_sp = (-LRU_C) * sp

    def convs(r):
        r0 = r * rb
        rows = slice(r0, r0 + rb)
        pa_ref[ha + r0:ha + r0 + rb, :] = us[rows, 256:512] * us[rows, 512:768]
        gb_ref[hb + r0:hb + r0 + rb, :] = us[rows, 768:1024] * _sigmoid(us[rows, 1024:1280])
        cx_ref[hc + r0:hc + r0 + rb, :] = us[rows, 1792:2304]
        ca = pa_ref[r0:r0 + rb, :] * caw_ref[0:1, :]
        for k in range(1, K_A):
            ca = ca + pa_ref[r0 + k * nb:r0 + k * nb + rb, :] * caw_ref[k:k + 1, :]
        ya = us[rows, 0:256] * ca
        ys[rows, 0:256] = _rms(ya, gg_ref[:, 0:256]).astype(BF16)
        cb = gb_ref[r0:r0 + rb, :] * cbw_ref[0:1, :]
        for k in range(1, K_B):
            cb = cb + gb_ref[r0 + k * nb:r0 + k * nb + rb, :] * cbw_ref[k:k + 1, :]
        cb = cb + cbb_ref[...]
        mu = jnp.mean(cb, axis=-1, keepdims=True)
        cc = cb - mu
        var = jnp.mean(cc * cc, axis=-1, keepdims=True)
        ln = cc * lax.rsqrt(var + LN_EPS) * lng_ref[...] + lnb_ref[...]
        yb = ln * _sigmoid(ln)
        ys[rows, 256:512] = _rms(yb, gg_ref[:, 256:512]).astype(BF16)
        xc = cx_ref[r0:r0 + rb, :] * ccw_ref[0:1, :]
        for k in range(1, K_C):
            xc = xc + cx_ref[r0 + k * nb:r0 + k * nb + rb, :] * ccw_ref[k:k + 1, :]
        xc = xc + ccb_ref[...]
        xc_ref[rows, :] = xc
        xcb_ref[rows, :] = xc.astype(BF16)
        if (r0 + rb) % gm == 0:
            grows = slice(r0 + rb - gm, r0 + rb)
            for half in range(2):
                cols = slice(half * 256, half * 256 + 256)
                g_ref[grows, cols] = jnp.dot(xcb_ref[grows, cols], wa_ref[half],
                                             preferred_element_type=F32)
                g_ref[grows, 512 + half * 256:768 + half * 256] = jnp.dot(
                    xcb_ref[grows, cols], wx_ref[half], preferred_element_type=F32)

    def coeffs(r):
        rows = slice(r * rb, r * rb + rb)
        rg = _sigmoid(g_ref[rows, 0:512] + ba_ref[...])
        ig = _sigmoid(g_ref[rows, 512:1024] + bx_ref[...])
        log_a = rg * neg_c_sp
        a = jnp.exp(log_a)
        v = jnp.maximum(jnp.tanh(-log_a) * (1.0 + a * a), 0.0)
        gain = jnp.where(v > 0.0, v * lax.rsqrt(v), v)
        a_ref[rows, :] = a
        b_ref[rows, :] = gain * (ig * xc_ref[rows, :])

    def gate(r):
        rows = slice(r * rb, r * rb + rb)
        cg = us[rows, 1280:1792]
        gelu = 0.5 * cg * (1.0 + jnp.tanh(0.7978845608028654 * (cg + 0.044715 * (cg * cg * cg))))
        yc = gelu * b_ref[rows, :]
        ys[rows, 512:1024] = _rms(yc, gg_ref[:, 512:1024]).astype(BF16)

    mm_units = ([functools.partial(in_proj, c) for c in range(n_in_proj)]
                + [functools.partial(out_proj, c) for c in range(n_out_proj)])
    ew_units = ([(functools.partial(convs, r), 5.0) for r in range(n_rb)]
                + [(functools.partial(coeffs, r), 1.5) for r in range(n_rb)])
    ew_total = sum(cost for _, cost in ew_units) + 1.0 * n_rb
    n_mm_done, ew_done = 0, 0.0

    def emit_mm(upto):
        nonlocal n_mm_done
        if pipelined:
            target = min(len(mm_units), int(upto * len(mm_units) + 0.999))
        else:
            target = n_in_proj if upto < 1.0 else len(mm_units)
        while n_mm_done < target:
            mm_units[n_mm_done]()
            n_mm_done += 1

    emit_mm(1.0 / len(mm_units))
    for unit, cost in ew_units:
        unit()
        ew_done += cost
        emit_mm(ew_done / ew_total)

    h = h_ref[...]
    for t in range(tt):
        rows = slice(t * nb, t * nb + nb)
        h = a_ref[rows, :] * h + b_ref[rows, :]
        b_ref[rows, :] = h

    for r in range(n_rb):
        gate(r)
        ew_done += 1.0
        emit_mm(ew_done / ew_total)
    emit_mm(1.0)

    ta = pa_ref[tm:tm + ha, :]
    tb = gb_ref[tm:tm + hb, :]
    tc = cx_ref[tm:tm + hc, :]
    pa_ref[0:ha, :] = ta
    gb_ref[0:hb, :] = tb
    cx_ref[0:hc, :] = tc
    h_ref[...] = h


def _mixer_call(x, init_a, init_b, init_c, init_h, layer, w, *, nb, tm):
    m = x.shape[0]
    n_tiles = m // tm
    ha, hb, hc = (K_A - 1) * nb, (K_B - 1) * nb, (K_C - 1) * nb
    rb = max(nb, 64)
    gm = min(tm, 256)
    init = lambda shape: pl.BlockSpec((None,) + shape, lambda i: (layer, 0, 0))
    per_channel = lambda c: pl.BlockSpec((DEPTH, c), lambda i: (0, 0))
    tile_new = pl.BlockSpec((tm, D_MODEL), lambda i: (jnp.minimum(i, n_tiles - 1), 0))
    tile_old = pl.BlockSpec((tm, D_MODEL), lambda i: (jnp.clip(i - 2, 0, n_tiles - 1), 0))
    return pl.pallas_call(
        functools.partial(_mixer_kernel, nb=nb, tm=tm, rb=rb, gm=gm, n_tiles=n_tiles, layer=layer),
        grid=(n_tiles + 2 if n_tiles > 1 else 1,),
        in_specs=[
            tile_new, tile_old,
            init((ha, W_A)), init((hb, W_B)), init((hc, W_C)), init((nb, W_C)),
            per_channel(D_MODEL),
            _const_spec((D_MODEL, IN_WIDTH), layer),
            _const_spec((K_A, W_A), layer),
            _const_spec((K_B, W_B), layer),
            per_channel(W_B),
            per_channel(W_B),
            per_channel(W_B),
            _const_spec((K_C, W_C), layer),
            per_channel(W_C),
            _const_spec((2, 256, 256), layer),
            per_channel(W_C),
            _const_spec((2, 256, 256), layer),
            per_channel(W_C),
            per_channel(W_C),
            per_channel(D_MODEL),
            _const_spec((D_MODEL, D_MODEL), layer),
        ],
        out_specs=[
            tile_old,
            init((ha, W_A)), init((hb, W_B)), init((hc, W_C)), init((nb, W_C)),
        ],
        out_shape=[jax.ShapeDtypeStruct((m, D_MODEL), F32)]
        + [jax.ShapeDtypeStruct(s.shape, F32) for s in (init_a, init_b, init_c, init_h)],
        input_output_aliases={2: 1, 3: 2, 4: 3, 5: 4},
        scratch_shapes=[
            pltpu.VMEM((tm, IN_WIDTH), F32),
            pltpu.VMEM((tm, IN_WIDTH), F32),
            pltpu.VMEM((ha + tm, W_A), F32),
            pltpu.VMEM((hb + tm, W_B), F32),
            pltpu.VMEM((hc + tm, W_C), F32),
            pltpu.VMEM((tm, W_C), F32),
            pltpu.VMEM((tm, W_C), BF16),
            pltpu.VMEM((tm, 2 * W_C), F32),
            pltpu.VMEM((tm, W_C), F32),
            pltpu.VMEM((tm, W_C), F32),
            pltpu.VMEM((nb, W_C), F32),
            pltpu.VMEM((tm, D_MODEL), BF16),
            pltpu.VMEM((tm, D_MODEL), BF16),
            pltpu.VMEM((tm, D_MODEL), BF16),
        ],
        compiler_params=pltpu.CompilerParams(
            dimension_semantics=("arbitrary",), vmem_limit_bytes=VMEM_LIMIT_BYTES),
        name="mixer",
    )(x, x, init_a, init_b, init_c, init_h, *w)


def _block_diag(w):
    half = LRU_HEADS // 2
    eye = jnp.eye(half, dtype=w.dtype)
    w = w.reshape(DEPTH, 2, half, LRU_HD, LRU_HD)
    full = jnp.einsum('lshij,hg->lshigj', w, eye)
    return full.reshape(DEPTH, 2, half * LRU_HD, half * LRU_HD).astype(BF16)


def kernel(x_prompt, x_sample, state_conv_a, state_conv_b, state_conv_c, state_lru_h, norm_ffn1, w1_up, w1_down, norm_mix, w_in, conv_a_w, conv_b_w, conv_b_b, ln_b_g, ln_b_b, conv_c_w, conv_c_b, lru_wa, lru_ba, lru_wx, lru_bx, lru_lam, grp_g, w_out, norm_ffn2, w2_up, w2_down, final_norm):
    bp, seq, _ = x_prompt.shape
    bs = x_sample.shape[0]

    p = {
        'g1': norm_ffn1, 'wup1': w1_up, 'wd1': w1_down,
        'g2': norm_ffn2, 'wup2': w2_up, 'wd2': w2_down,
        'gf': final_norm[None, :],
        'mix': (norm_mix, w_in.astype(BF16), conv_a_w, conv_b_w, conv_b_b, ln_b_g, ln_b_b,
                conv_c_w, conv_c_b, _block_diag(lru_wa), lru_ba, _block_diag(lru_wx), lru_bx,
                lru_lam, grp_g, w_out.astype(BF16)),
    }

    tmaj = lambda s: s.transpose(0, 2, 1, 3)
    rows = lambda s: s.reshape(s.shape[0], -1, s.shape[-1])
    p_st = (jnp.zeros((DEPTH, (K_A - 1) * bp, W_A), F32), jnp.zeros((DEPTH, (K_B - 1) * bp, W_B), F32),
            jnp.zeros((DEPTH, (K_C - 1) * bp, W_C), F32), jnp.zeros((DEPTH, bp, W_C), F32))
    s_st = (rows(tmaj(state_conv_a)), rows(tmaj(state_conv_b)), rows(tmaj(state_conv_c)), state_lru_h)
    xp, xs = x_prompt, x_sample.reshape(bs, D_MODEL)
    for l in range(DEPTH):
        xp, xs = _ffn_call(xp, xs, l, p['g1'], p['wup1'], p['wd1'], p['gf'], tm=TM_FFN, final=False,
                           layout='from_seq' if l == 0 else 'rows', nb=bp)
        xp, *p_st = _mixer_call(xp, *p_st, l, p['mix'], nb=bp, tm=TM_PROMPT)
        xs, *s_st = _mixer_call(xs, *s_st, l, p['mix'], nb=bs, tm=bs)
        xp, xs = _ffn_call(xp, xs, l, p['g2'], p['wup2'], p['wd2'], p['gf'], tm=TM_FFN,
                           final=(l == DEPTH - 1), layout='to_seq' if l == DEPTH - 1 else 'rows', nb=bp)
    y_prompt, ys = xp, xs

    def seq_major(st, nb):
        a, b, c, h = st
        unrow = lambda s, k: tmaj(s.reshape(DEPTH, k - 1, nb, s.shape[-1]))
        return unrow(a, K_A), unrow(b, K_B), unrow(c, K_C), h

    pa, pb, pc, ph = seq_major(p_st, bp)
    sa, sb, sc, sh = seq_major(s_st, bs)
    y_sample = ys.reshape(bs, 1, D_MODEL)
    return (y_prompt, y_sample, pa, pb, pc, ph, sa, sb, sc, sh)
```

```python
import functools

import jax
import jax.numpy as jnp
from jax import lax
from jax.experimental import pallas as pl
from jax.experimental.pallas import tpu as pltpu

D_MODEL = 1024
DEPTH = 4
W_A = 256
W_B = 256
W_C = 512
K_A = 3
K_B = 31
K_C = 4
LRU_HEADS = 8
LRU_HD = W_C // LRU_HEADS
LRU_C = 8.0
D_FF = 2816
RMS_EPS = 1e-6
LN_EPS = 1e-5
IN_WIDTH = 3 * W_A + 2 * W_B + 2 * W_C


def _column_slices(widths):
    bounds = [sum(widths[:k]) for k in range(len(widths) + 1)]
    return tuple(slice(lo, hi) for lo, hi in zip(bounds[:-1], bounds[1:]))


COL_A_B, COL_A_C, COL_A_X, COL_B_V, COL_B_G, COL_C_G, COL_C_X = _column_slices(
    (W_A, W_A, W_A, W_B, W_B, W_C, W_C))
OUT_A, OUT_B, OUT_C = _column_slices((W_A, W_B, W_C))
GATE_R, GATE_I = _column_slices((W_C, W_C))
HALF_C = W_C // 2

FF_CHUNK = 256
N_FF_CHUNKS = D_FF // FF_CHUNK
MM_CHUNK = 256
TM_PROMPT = 512
TM_FFN = 1024
SUBLANES = 8
VMEM_LIMIT_BYTES = 62 * 1024 * 1024

F32 = jnp.float32
BF16 = jnp.bfloat16


def _rms(x, g):
    return x * lax.rsqrt(jnp.mean(x * x, axis=-1, keepdims=True) + RMS_EPS) * g


def _sigmoid(x):
    return 1.0 / (1.0 + jnp.exp(-x))


def _const_spec(shape, layer):
    nd = len(shape)
    return pl.BlockSpec((None,) + tuple(shape), lambda i: (layer,) + (0,) * nd,
                        pipeline_mode=pl.Buffered(1))


def _swiglu_half_step(x, g_ref, wup_ref, wd_ref, gf_ref, final):
    h = _rms(x, g_ref[...]).astype(BF16)
    acc = jnp.zeros(x.shape, F32)
    for c in range(N_FF_CHUNKS):
        lo = c * FF_CHUNK
        wg = wup_ref[:, lo:lo + FF_CHUNK].astype(BF16)
        wu = wup_ref[:, D_FF + lo:D_FF + lo + FF_CHUNK].astype(BF16)
        wd = wd_ref[lo:lo + FF_CHUNK, :].astype(BF16)
        g = jnp.dot(h, wg, preferred_element_type=F32)
        u = jnp.dot(h, wu, preferred_element_type=F32)
        a = (g * _sigmoid(g) * u).astype(BF16)
        acc = acc + jnp.dot(a, wd, preferred_element_type=F32)
    y = x + 0.5 * acc
    if final:
        y = _rms(y, gf_ref[...])
    return y


def _ffn_kernel(x_ref, xs_ref, g_ref, wup_ref, wd_ref, gf_ref, o_ref, os_ref, *, final, layout, nb, layer):
    g_ref = g_ref.at[pl.ds(layer, 1)]
    if layout == 'from_seq':
        tq = x_ref.shape[1]
        x = x_ref[...].reshape(nb * tq, D_MODEL)
    else:
        x = x_ref[...]
    y = _swiglu_half_step(x, g_ref, wup_ref, wd_ref, gf_ref, final)
    if layout == 'from_seq':
        o_ref[...] = jnp.swapaxes(y.reshape(nb, tq, D_MODEL), 0, 1).reshape(tq * nb, D_MODEL)
    elif layout == 'to_seq':
        o_ref[...] = jnp.swapaxes(y.reshape(y.shape[0] // nb, nb, D_MODEL), 0, 1)
    else:
        o_ref[...] = y

    @pl.when(pl.program_id(0) == pl.num_programs(0) - 1)
    def _():
        os_ref[...] = _swiglu_half_step(xs_ref[...], g_ref, wup_ref, wd_ref, gf_ref, final)


def _ffn_call(x, xs, layer, norm_g, wup, wd, final_g, *, tm, final, layout='rows', nb=1):
    m = x.shape[0] * x.shape[1] if layout == 'from_seq' else x.shape[0]
    ms = xs.shape[0]
    rows_spec = pl.BlockSpec((tm, D_MODEL), lambda i: (i, 0))
    seq_spec = pl.BlockSpec((nb, tm // nb, D_MODEL), lambda i: (0, i, 0))
    sample_spec = pl.BlockSpec((ms, D_MODEL), lambda i: (0, 0))
    return pl.pallas_call(
        functools.partial(_ffn_kernel, final=final, layout=layout, nb=nb, layer=layer),
        grid=(m // tm,),
        in_specs=[
            seq_spec if layout == 'from_seq' else rows_spec,
            sample_spec,
            pl.BlockSpec((DEPTH, D_MODEL), lambda i: (0, 0)),
            _const_spec((D_MODEL, 2 * D_FF), layer),
            _const_spec((D_FF, D_MODEL), layer),
            pl.BlockSpec((1, D_MODEL), lambda i: (0, 0)),
        ],
        out_specs=[seq_spec if layout == 'to_seq' else rows_spec, sample_spec],
        out_shape=[
            jax.ShapeDtypeStruct((nb, m // nb, D_MODEL) if layout == 'to_seq' else (m, D_MODEL), F32),
            jax.ShapeDtypeStruct((ms, D_MODEL), F32),
        ],
        compiler_params=pltpu.CompilerParams(
            dimension_semantics=("arbitrary",), vmem_limit_bytes=VMEM_LIMIT_BYTES),
        name="ffn",
    )(x, xs, norm_g, wup, wd, final_g)


def _mixer_kernel(x_ref, xr_ref, ia_ref, ib_ref, ic_ref, ih_ref,
                  gmix_ref, win_ref, caw_ref, cbw_ref, cbb_ref, lng_ref, lnb_ref,
                  ccw_ref, ccb_ref, wa_ref, ba_ref, wx_ref, bx_ref, lam_ref, gg_ref, wout_ref,
                  o_ref, oa_ref, ob_ref, oc_ref, oh_ref,
                  u0_ref, u1_ref, pa_ref, gb_ref, cx_ref, xc_ref, xcb_ref, g_ref, a_ref, b_ref, h_ref,
                  y0_ref, y1_ref, hx_ref, *, nb, tm, rb, gm, n_tiles, layer):
    i = pl.program_id(0)
    gmix_ref, cbb_ref, lng_ref, lnb_ref, ccb_ref, ba_ref, bx_ref, lam_ref, gg_ref = (
        r.at[pl.ds(layer, 1)] for r in (gmix_ref, cbb_ref, lng_ref, lnb_ref, ccb_ref, ba_ref, bx_ref,
                                        lam_ref, gg_ref))
    ha, hb, hc = (K_A - 1) * nb, (K_B - 1) * nb, (K_C - 1) * nb
    pipelined = n_tiles > 1

    @pl.when(i == 0)
    def _():
        pa_ref[0:ha, :] = ia_ref[...]
        gb_ref[0:hb, :] = ib_ref[...]
        cx_ref[0:hc, :] = ic_ref[...]
        h_ref[...] = ih_ref[...]
        if pipelined:
            y1_ref[...] = jnp.zeros((tm, D_MODEL), BF16)

    step = functools.partial(
        _mixer_step, x_ref, xr_ref, gmix_ref, win_ref, caw_ref, cbw_ref, cbb_ref, lng_ref, lnb_ref,
        ccw_ref, ccb_ref, wa_ref, ba_ref, wx_ref, bx_ref, lam_ref, gg_ref, wout_ref, o_ref,
        pa_ref, gb_ref, cx_ref, xc_ref, xcb_ref, g_ref, a_ref, b_ref, h_ref, hx_ref,
        nb=nb, tm=tm, rb=rb, gm=gm)

    if pipelined:
        slots = ((u0_ref, u1_ref, y0_ref, y1_ref), (u1_ref, u0_ref, y1_ref, y0_ref))
        last = n_tiles + 1
        busy = jnp.logical_and(i >= 1, i <= n_tiles)

        @pl.when(i == 0)
        def _():
            step(*slots[0], stages='in')

        @pl.when(jnp.logical_and(busy, i % 2 == 0))
        def _():
            step(*slots[0], stages='all')

        @pl.when(jnp.logical_and(busy, i % 2 == 1))
        def _():
            step(*slots[1], stages='all')

        @pl.when(i == last)
        def _():
            step(*slots[last % 2], stages='out')
    else:
        step(u0_ref, u0_ref, y0_ref, y0_ref, stages='in_order')

    @pl.when(i == pl.num_programs(0) - 1)
    def _():
        oa_ref[...] = pa_ref[0:ha, :]
        ob_ref[...] = gb_ref[0:hb, :]
        oc_ref[...] = cx_ref[0:hc, :]
        oh_ref[...] = h_ref[...]


def _mixer_step(x_ref, xr_ref, gmix_ref, win_ref, caw_ref, cbw_ref, cbb_ref, lng_ref, lnb_ref,
                ccw_ref, ccb_ref, wa_ref, ba_ref, wx_ref, bx_ref, lam_ref, gg_ref, wout_ref, o_ref,
                pa_ref, gb_ref, cx_ref, xc_ref, xcb_ref, g_ref, a_ref, b_ref, h_ref, hx_ref,
                u_new, us, y_old, ys, *, nb, tm, rb, gm, stages):
    ha, hb, hc = (K_A - 1) * nb, (K_B - 1) * nb, (K_C - 1) * nb
    n_rb = tm // rb
    tt = tm // nb
    n_in_proj, n_out_proj = IN_WIDTH // MM_CHUNK, D_MODEL // MM_CHUNK
    pipelined = stages == 'all'

    def in_proj(c):
        cols = slice(c * MM_CHUNK, (c + 1) * MM_CHUNK)
        u_new[:, cols] = jnp.dot(hx_ref[...], win_ref[:, cols], preferred_element_type=F32)

    def out_proj(c):
        cols = slice(c * MM_CHUNK, (c + 1) * MM_CHUNK)
        o_ref[:, cols] = xr_ref[:, cols] + jnp.dot(y_old[...], wout_ref[:, cols],
                                                   preferred_element_type=F32)

    if stages == 'out':
        for c in range(n_out_proj):
            out_proj(c)
        return

    hx_ref[...] = _rms(x_ref[...], gmix_ref[...]).astype(BF16)
    if stages == 'in':
        for c in range(n_in_proj):
            in_proj(c)
        return

    sp = lam_ref[...]
    sp = jnp.maximum(-sp, 0.0) + jnp.log1p(jnp.exp(-jnp.abs(sp)))
    neg_c_sp = (-LRU_C) * sp

    def convs(r):
        r0 = r * rb
        rows = slice(r0, r0 + rb)
        pa_ref[ha + r0:ha + r0 + rb, :] = us[rows, COL_A_C] * us[rows, COL_A_X]
        gb_ref[hb + r0:hb + r0 + rb, :] = us[rows, COL_B_V] * _sigmoid(us[rows, COL_B_G])
        cx_ref[hc + r0:hc + r0 + rb, :] = us[rows, COL_C_X]
        ca = pa_ref[r0:r0 + rb, :] * caw_ref[0:1, :]
        for k in range(1, K_A):
            ca = ca + pa_ref[r0 + k * nb:r0 + k * nb + rb, :] * caw_ref[k:k + 1, :]
        ya = us[rows, COL_A_B] * ca
        ys[rows, OUT_A] = _rms(ya, gg_ref[:, OUT_A]).astype(BF16)
        cb = gb_ref[r0:r0 + rb, :] * cbw_ref[0:1, :]
        for k in range(1, K_B):
            cb = cb + gb_ref[r0 + k * nb:r0 + k * nb + rb, :] * cbw_ref[k:k + 1, :]
        cb = cb + cbb_ref[...]
        mu = jnp.mean(cb, axis=-1, keepdims=True)
        cc = cb - mu
        var = jnp.mean(cc * cc, axis=-1, keepdims=True)
        ln = cc * lax.rsqrt(var + LN_EPS) * lng_ref[...] + lnb_ref[...]
        yb = ln * _sigmoid(ln)
        ys[rows, OUT_B] = _rms(yb, gg_ref[:, OUT_B]).astype(BF16)
        xc = cx_ref[r0:r0 + rb, :] * ccw_ref[0:1, :]
        for k in range(1, K_C):
            xc = xc + cx_ref[r0 + k * nb:r0 + k * nb + rb, :] * ccw_ref[k:k + 1, :]
        xc = xc + ccb_ref[...]
        xc_ref[rows, :] = xc
        xcb_ref[rows, :] = xc.astype(BF16)
        if (r0 + rb) % gm == 0:
            grows = slice(r0 + rb - gm, r0 + rb)
            for half in range(2):
                cols = slice(half * HALF_C, (half + 1) * HALF_C)
                g_ref[grows, cols] = jnp.dot(xcb_ref[grows, cols], wa_ref[half],
                                             preferred_element_type=F32)
                g_ref[grows, W_C + half * HALF_C:W_C + (half + 1) * HALF_C] = jnp.dot(
                    xcb_ref[grows, cols], wx_ref[half], preferred_element_type=F32)

    def coeffs(r):
        rows = slice(r * rb, r * rb + rb)
        rg = _sigmoid(g_ref[rows, GATE_R] + ba_ref[...])
        ig = _sigmoid(g_ref[rows, GATE_I] + bx_ref[...])
        log_a = rg * neg_c_sp
        a = jnp.exp(log_a)
        v = jnp.maximum(jnp.tanh(-log_a) * (1.0 + a * a), 0.0)
        gain = jnp.where(v > 0.0, v * lax.rsqrt(v), v)
        a_ref[rows, :] = a
        b_ref[rows, :] = gain * (ig * xc_ref[rows, :])

    def gate(r):
        rows = slice(r * rb, r * rb + rb)
        cg = us[rows, COL_C_G]
        gelu = 0.5 * cg * (1.0 + jnp.tanh(0.7978845608028654 * (cg + 0.044715 * (cg * cg * cg))))
        yc = gelu * b_ref[rows, :]
        ys[rows, OUT_C] = _rms(yc, gg_ref[:, OUT_C]).astype(BF16)

    mm_units = ([functools.partial(in_proj, c) for c in range(n_in_proj)]
                + [functools.partial(out_proj, c) for c in range(n_out_proj)])
    ew_units = ([(functools.partial(convs, r), 5.0) for r in range(n_rb)]
                + [(functools.partial(coeffs, r), 1.5) for r in range(n_rb)])
    ew_total = sum(cost for _, cost in ew_units) + 1.0 * n_rb
    n_mm_done, ew_done = 0, 0.0

    def emit_mm(upto):
        nonlocal n_mm_done
        if pipelined:
            target = min(len(mm_units), int(upto * len(mm_units) + 0.999))
        else:
            target = n_in_proj if upto < 1.0 else len(mm_units)
        while n_mm_done < target:
            mm_units[n_mm_done]()
            n_mm_done += 1

    emit_mm(1.0 / len(mm_units))
    for unit, cost in ew_units:
        unit()
        ew_done += cost
        emit_mm(ew_done / ew_total)

    h = h_ref[...]
    for t in range(tt):
        rows = slice(t * nb, t * nb + nb)
        h = a_ref[rows, :] * h + b_ref[rows, :]
        b_ref[rows, :] = h

    for r in range(n_rb):
        gate(r)
        ew_done += 1.0
        emit_mm(ew_done / ew_total)
    emit_mm(1.0)

    ta = pa_ref[tm:tm + ha, :]
    tb = gb_ref[tm:tm + hb, :]
    tc = cx_ref[tm:tm + hc, :]
    pa_ref[0:ha, :] = ta
    gb_ref[0:hb, :] = tb
    cx_ref[0:hc, :] = tc
    h_ref[...] = h


def _mixer_call(x, init_a, init_b, init_c, init_h, layer, w, *, nb, tm):
    m = x.shape[0]
    n_tiles = m // tm
    ha, hb, hc = (K_A - 1) * nb, (K_B - 1) * nb, (K_C - 1) * nb
    rb = max(nb, 64)
    gm = min(tm, 256)
    init = lambda shape: pl.BlockSpec((None,) + shape, lambda i: (layer, 0, 0))
    per_channel = lambda c: pl.BlockSpec((DEPTH, c), lambda i: (0, 0))
    tile_new = pl.BlockSpec((tm, D_MODEL), lambda i: (jnp.minimum(i, n_tiles - 1), 0))
    tile_old = pl.BlockSpec((tm, D_MODEL), lambda i: (jnp.clip(i - 2, 0, n_tiles - 1), 0))
    return pl.pallas_call(
        functools.partial(_mixer_kernel, nb=nb, tm=tm, rb=rb, gm=gm, n_tiles=n_tiles, layer=layer),
        grid=(n_tiles + 2 if n_tiles > 1 else 1,),
        in_specs=[
            tile_new, tile_old,
            init((ha, W_A)), init((hb, W_B)), init((hc, W_C)), init((nb, W_C)),
            per_channel(D_MODEL),
            _const_spec((D_MODEL, IN_WIDTH), layer),
            _const_spec((K_A, W_A), layer),
            _const_spec((K_B, W_B), layer),
            per_channel(W_B),
            per_channel(W_B),
            per_channel(W_B),
            _const_spec((K_C, W_C), layer),
            per_channel(W_C),
            _const_spec((2, HALF_C, HALF_C), layer),
            per_channel(W_C),
            _const_spec((2, HALF_C, HALF_C), layer),
            per_channel(W_C),
            per_channel(W_C),
            per_channel(D_MODEL),
            _const_spec((D_MODEL, D_MODEL), layer),
        ],
        out_specs=[
            tile_old,
            init((ha, W_A)), init((hb, W_B)), init((hc, W_C)), init((nb, W_C)),
        ],
        out_shape=[jax.ShapeDtypeStruct((m, D_MODEL), F32)]
        + [jax.ShapeDtypeStruct(s.shape, F32) for s in (init_a, init_b, init_c, init_h)],
        input_output_aliases={2: 1, 3: 2, 4: 3, 5: 4},
        scratch_shapes=[
            pltpu.VMEM((tm, IN_WIDTH), F32),
            pltpu.VMEM((tm, IN_WIDTH), F32),
            pltpu.VMEM((ha + tm, W_A), F32),
            pltpu.VMEM((hb + tm, W_B), F32),
            pltpu.VMEM((hc + tm, W_C), F32),
            pltpu.VMEM((tm, W_C), F32),
            pltpu.VMEM((tm, W_C), BF16),
            pltpu.VMEM((tm, 2 * W_C), F32),
            pltpu.VMEM((tm, W_C), F32),
            pltpu.VMEM((tm, W_C), F32),
            pltpu.VMEM((nb, W_C), F32),
            pltpu.VMEM((tm, D_MODEL), BF16),
            pltpu.VMEM((tm, D_MODEL), BF16),
            pltpu.VMEM((tm, D_MODEL), BF16),
        ],
        compiler_params=pltpu.CompilerParams(
            dimension_semantics=("arbitrary",), vmem_limit_bytes=VMEM_LIMIT_BYTES),
        name="mixer",
    )(x, x, init_a, init_b, init_c, init_h, *w)


def _block_diag(w):
    half = LRU_HEADS // 2
    eye = jnp.eye(half, dtype=w.dtype)
    w = w.reshape(DEPTH, 2, half, LRU_HD, LRU_HD)
    full = jnp.einsum('lshij,hg->lshigj', w, eye)
    return full.reshape(DEPTH, 2, half * LRU_HD, half * LRU_HD).astype(BF16)


def kernel(x_prompt, x_sample, state_conv_a, state_conv_b, state_conv_c, state_lru_h, norm_ffn1, w1_up, w1_down, norm_mix, w_in, conv_a_w, conv_b_w, conv_b_b, ln_b_g, ln_b_b, conv_c_w, conv_c_b, lru_wa, lru_ba, lru_wx, lru_bx, lru_lam, grp_g, w_out, norm_ffn2, w2_up, w2_down, final_norm):
    bp, seq, _ = x_prompt.shape
    bs = x_sample.shape[0]

    p = {
        'g1': norm_ffn1, 'wup1': w1_up, 'wd1': w1_down,
        'g2': norm_ffn2, 'wup2': w2_up, 'wd2': w2_down,
        'gf': final_norm[None, :],
        'mix': (norm_mix, w_in.astype(BF16), conv_a_w, conv_b_w, conv_b_b, ln_b_g, ln_b_b,
                conv_c_w, conv_c_b, _block_diag(lru_wa), lru_ba, _block_diag(lru_wx), lru_bx,
                lru_lam, grp_g, w_out.astype(BF16)),
    }

    tmaj = lambda s: s.transpose(0, 2, 1, 3)
    rows = lambda s: s.reshape(s.shape[0], -1, s.shape[-1])
    p_st = (jnp.zeros((DEPTH, (K_A - 1) * bp, W_A), F32), jnp.zeros((DEPTH, (K_B - 1) * bp, W_B), F32),
            jnp.zeros((DEPTH, (K_C - 1) * bp, W_C), F32), jnp.zeros((DEPTH, bp, W_C), F32))
    s_st = (rows(tmaj(state_conv_a)), rows(tmaj(state_conv_b)), rows(tmaj(state_conv_c)), state_lru_h)
    xp, xs = x_prompt, x_sample.reshape(bs, D_MODEL)
    for l in range(DEPTH):
        xp, xs = _ffn_call(xp, xs, l, p['g1'], p['wup1'], p['wd1'], p['gf'], tm=TM_FFN, final=False,
                           layout='from_seq' if l == 0 else 'rows', nb=bp)
        xp, *p_st = _mixer_call(xp, *p_st, l, p['mix'], nb=bp, tm=TM_PROMPT)
        xs, *s_st = _mixer_call(xs, *s_st, l, p['mix'], nb=bs, tm=bs)
        xp, xs = _ffn_call(xp, xs, l, p['g2'], p['wup2'], p['wd2'], p['gf'], tm=TM_FFN,
                           final=(l == DEPTH - 1), layout='to_seq' if l == DEPTH - 1 else 'rows', nb=bp)
    y_prompt, ys = xp, xs

    def seq_major(st, nb):
        a, b, c, h = st
        unrow = lambda s, k: tmaj(s.reshape(DEPTH, k - 1, nb, s.shape[-1]))
        return unrow(a, K_A), unrow(b, K_B), unrow(c, K_C), h

    pa, pb, pc, ph = seq_major(p_st, bp)
    sa, sb, sc, sh = seq_major(s_st, bs)
    y_sample = ys.reshape(bs, 1, D_MODEL)
    return (y_prompt, y_sample, pa, pb, pc, ph, sa, sb, sc, sh)
```

```python
import functools

import jax
import jax.numpy as jnp
from jax import lax
from jax.experimental import pallas as pl
from jax.experimental.pallas import tpu as pltpu

D_MODEL = 1024
DEPTH = 4
W_A = 256
W_B = 256
W_C = 512
K_A = 3
K_B = 31
K_C = 4
LRU_HEADS = 8
LRU_HD = W_C // LRU_HEADS
LRU_C = 8.0
D_FF = 2816
RMS_EPS = 1e-6
LN_EPS = 1e-5
IN_WIDTH = 3 * W_A + 2 * W_B + 2 * W_C


def _column_slices(widths):
    bounds = [sum(widths[:k]) for k in range(len(widths) + 1)]
    return tuple(slice(lo, hi) for lo, hi in zip(bounds[:-1], bounds[1:]))


COL_A_B, COL_A_C, COL_A_X, COL_B_V, COL_B_G, COL_C_G, COL_C_X = _column_slices(
    (W_A, W_A, W_A, W_B, W_B, W_C, W_C))
OUT_A, OUT_B, OUT_C = _column_slices((W_A, W_B, W_C))
GATE_R, GATE_I = _column_slices((W_C, W_C))
HALF_C = W_C // 2

FF_CHUNK = 256
N_FF_CHUNKS = D_FF // FF_CHUNK
MM_CHUNK = 256
TM_PROMPT = 512
TM_FFN = 1024
SUBLANES = 8
VMEM_LIMIT_BYTES = 62 * 1024 * 1024

F32 = jnp.float32
BF16 = jnp.bfloat16


def _rms(x, g):
    return x * lax.rsqrt(jnp.mean(x * x, axis=-1, keepdims=True) + RMS_EPS) * g


def _sigmoid(x):
    return 1.0 / (1.0 + jnp.exp(-x))


def _const_spec(shape, layer):
    nd = len(shape)
    return pl.BlockSpec((None,) + tuple(shape), lambda i: (layer,) + (0,) * nd,
                        pipeline_mode=pl.Buffered(1))


def _swiglu_half_step(x, g_ref, wup_ref, wd_ref, gf_ref, final):
    h = _rms(x, g_ref[...]).astype(BF16)
    acc = jnp.zeros(x.shape, F32)
    for c in range(N_FF_CHUNKS):
        lo = c * FF_CHUNK
        wg = wup_ref[:, lo:lo + FF_CHUNK].astype(BF16)
        wu = wup_ref[:, D_FF + lo:D_FF + lo + FF_CHUNK].astype(BF16)
        wd = wd_ref[lo:lo + FF_CHUNK, :].astype(BF16)
        g = jnp.dot(h, wg, preferred_element_type=F32)
        u = jnp.dot(h, wu, preferred_element_type=F32)
        a = (g * _sigmoid(g) * u).astype(BF16)
        acc = acc + jnp.dot(a, wd, preferred_element_type=F32)
    y = x + 0.5 * acc
    if final:
        y = _rms(y, gf_ref[...])
    return y


def _ffn_kernel(x_ref, xs_ref, g_ref, wup_ref, wd_ref, gf_ref, o_ref, os_ref, *, final, layout, nb, layer):
    g_ref = g_ref.at[pl.ds(layer, 1)]
    if layout == 'from_seq':
        tq = x_ref.shape[1]
        x = x_ref[...].reshape(nb * tq, D_MODEL)
    else:
        x = x_ref[...]
    y = _swiglu_half_step(x, g_ref, wup_ref, wd_ref, gf_ref, final)
    if layout == 'from_seq':
        o_ref[...] = jnp.swapaxes(y.reshape(nb, tq, D_MODEL), 0, 1).reshape(tq * nb, D_MODEL)
    elif layout == 'to_seq':
        o_ref[...] = jnp.swapaxes(y.reshape(y.shape[0] // nb, nb, D_MODEL), 0, 1)
    else:
        o_ref[...] = y

    @pl.when(pl.program_id(0) == pl.num_programs(0) - 1)
    def _():
        os_ref[...] = _swiglu_half_step(xs_ref[...], g_ref, wup_ref, wd_ref, gf_ref, final)


def _ffn_call(x, xs, layer, norm_g, wup, wd, final_g, *, tm, final, layout='rows', nb=1):
    m = x.shape[0] * x.shape[1] if layout == 'from_seq' else x.shape[0]
    ms = xs.shape[0]
    rows_spec = pl.BlockSpec((tm, D_MODEL), lambda i: (i, 0))
    seq_spec = pl.BlockSpec((nb, tm // nb, D_MODEL), lambda i: (0, i, 0))
    sample_spec = pl.BlockSpec((ms, D_MODEL), lambda i: (0, 0))
    return pl.pallas_call(
        functools.partial(_ffn_kernel, final=final, layout=layout, nb=nb, layer=layer),
        grid=(m // tm,),
        in_specs=[
            seq_spec if layout == 'from_seq' else rows_spec,
            sample_spec,
            pl.BlockSpec((DEPTH, D_MODEL), lambda i: (0, 0)),
            _const_spec((D_MODEL, 2 * D_FF), layer),
            _const_spec((D_FF, D_MODEL), layer),
            pl.BlockSpec((1, D_MODEL), lambda i: (0, 0)),
        ],
        out_specs=[seq_spec if layout == 'to_seq' else rows_spec, sample_spec],
        out_shape=[
            jax.ShapeDtypeStruct((nb, m // nb, D_MODEL) if layout == 'to_seq' else (m, D_MODEL), F32),
            jax.ShapeDtypeStruct((ms, D_MODEL), F32),
        ],
        compiler_params=pltpu.CompilerParams(
            dimension_semantics=("arbitrary",), vmem_limit_bytes=VMEM_LIMIT_BYTES),
        name="ffn",
    )(x, xs, norm_g, wup, wd, final_g)


def _mixer_kernel(x_ref, xr_ref, ia_ref, ib_ref, ic_ref, ih_ref,
                  gmix_ref, win_ref, caw_ref, cbw_ref, cbb_ref, lng_ref, lnb_ref,
                  ccw_ref, ccb_ref, wa_ref, ba_ref, wx_ref, bx_ref, lam_ref, gg_ref, wout_ref,
                  o_ref, oa_ref, ob_ref, oc_ref, oh_ref,
                  u0_ref, u1_ref, pa_ref, gb_ref, cx_ref, xc_ref, xcb_ref, g_ref, a_ref, b_ref, h_ref,
                  y0_ref, y1_ref, hx_ref, *, nb, tm, rb, gm, n_tiles, layer):
    i = pl.program_id(0)
    gmix_ref, cbb_ref, lng_ref, lnb_ref, ccb_ref, ba_ref, bx_ref, lam_ref, gg_ref = (
        r.at[pl.ds(layer, 1)] for r in (gmix_ref, cbb_ref, lng_ref, lnb_ref, ccb_ref, ba_ref, bx_ref,
                                        lam_ref, gg_ref))
    ha, hb, hc = (K_A - 1) * nb, (K_B - 1) * nb, (K_C - 1) * nb
    pipelined = n_tiles > 1

    @pl.when(i == 0)
    def _():
        pa_ref[0:ha, :] = ia_ref[...]
        gb_ref[0:hb, :] = ib_ref[...]
        cx_ref[0:hc, :] = ic_ref[...]
        h_ref[...] = ih_ref[...]

    step = functools.partial(
        _mixer_step, x_ref, xr_ref, gmix_ref, win_ref, caw_ref, cbw_ref, cbb_ref, lng_ref, lnb_ref,
        ccw_ref, ccb_ref, wa_ref, ba_ref, wx_ref, bx_ref, lam_ref, gg_ref, wout_ref, o_ref,
        pa_ref, gb_ref, cx_ref, xc_ref, xcb_ref, g_ref, a_ref, b_ref, h_ref, hx_ref,
        nb=nb, tm=tm, rb=rb, gm=gm)

    if pipelined:
        slots = ((u0_ref, u1_ref, y0_ref, y1_ref), (u1_ref, u0_ref, y1_ref, y0_ref))
        last = n_tiles + 1
        steady = jnp.logical_and(i >= 2, i <= n_tiles - 1)

        @pl.when(i == 0)
        def _():
            step(*slots[0], stages='in')

        @pl.when(i == 1)
        def _():
            step(*slots[1], stages='no_out')

        @pl.when(jnp.logical_and(steady, i % 2 == 0))
        def _():
            step(*slots[0], stages='all')

        @pl.when(jnp.logical_and(steady, i % 2 == 1))
        def _():
            step(*slots[1], stages='all')

        @pl.when(i == n_tiles)
        def _():
            step(*slots[n_tiles % 2], stages='no_in')

        @pl.when(i == last)
        def _():
            step(*slots[last % 2], stages='out')
    else:
        step(u0_ref, u0_ref, y0_ref, y0_ref, stages='in_order')

    @pl.when(i == pl.num_programs(0) - 1)
    def _():
        oa_ref[...] = pa_ref[0:ha, :]
        ob_ref[...] = gb_ref[0:hb, :]
        oc_ref[...] = cx_ref[0:hc, :]
        oh_ref[...] = h_ref[...]


def _mixer_step(x_ref, xr_ref, gmix_ref, win_ref, caw_ref, cbw_ref, cbb_ref, lng_ref, lnb_ref,
                ccw_ref, ccb_ref, wa_ref, ba_ref, wx_ref, bx_ref, lam_ref, gg_ref, wout_ref, o_ref,
                pa_ref, gb_ref, cx_ref, xc_ref, xcb_ref, g_ref, a_ref, b_ref, h_ref, hx_ref,
                u_new, us, y_old, ys, *, nb, tm, rb, gm, stages):
    ha, hb, hc = (K_A - 1) * nb, (K_B - 1) * nb, (K_C - 1) * nb
    n_rb = tm // rb
    tt = tm // nb
    n_in_proj, n_out_proj = IN_WIDTH // MM_CHUNK, D_MODEL // MM_CHUNK
    pipelined = stages in ('all', 'no_out', 'no_in')

    def in_proj(c):
        cols = slice(c * MM_CHUNK, (c + 1) * MM_CHUNK)
        u_new[:, cols] = jnp.dot(hx_ref[...], win_ref[:, cols], preferred_element_type=F32)

    def out_proj(c):
        cols = slice(c * MM_CHUNK, (c + 1) * MM_CHUNK)
        o_ref[:, cols] = xr_ref[:, cols] + jnp.dot(y_old[...], wout_ref[:, cols],
                                                   preferred_element_type=F32)

    if stages == 'out':
        for c in range(n_out_proj):
            out_proj(c)
        return

    if stages != 'no_in':
        hx_ref[...] = _rms(x_ref[...], gmix_ref[...]).astype(BF16)
    if stages == 'in':
        for c in range(n_in_proj):
            in_proj(c)
        return

    sp = lam_ref[...]
    sp = jnp.maximum(-sp, 0.0) + jnp.log1p(jnp.exp(-jnp.abs(sp)))
    neg_c_sp = (-LRU_C) * sp

    def convs(r):
        r0 = r * rb
        rows = slice(r0, r0 + rb)
        pa_ref[ha + r0:ha + r0 + rb, :] = us[rows, COL_A_C] * us[rows, COL_A_X]
        gb_ref[hb + r0:hb + r0 + rb, :] = us[rows, COL_B_V] * _sigmoid(us[rows, COL_B_G])
        cx_ref[hc + r0:hc + r0 + rb, :] = us[rows, COL_C_X]
        ca = pa_ref[r0:r0 + rb, :] * caw_ref[0:1, :]
        for k in range(1, K_A):
            ca = ca + pa_ref[r0 + k * nb:r0 + k * nb + rb, :] * caw_ref[k:k + 1, :]
        ya = us[rows, COL_A_B] * ca
        ys[rows, OUT_A] = _rms(ya, gg_ref[:, OUT_A]).astype(BF16)
        cb = gb_ref[r0:r0 + rb, :] * cbw_ref[0:1, :]
        for k in range(1, K_B):
            cb = cb + gb_ref[r0 + k * nb:r0 + k * nb + rb, :] * cbw_ref[k:k + 1, :]
        cb = cb + cbb_ref[...]
        mu = jnp.mean(cb, axis=-1, keepdims=True)
        cc = cb - mu
        var = jnp.mean(cc * cc, axis=-1, keepdims=True)
        ln = cc * lax.rsqrt(var + LN_EPS) * lng_ref[...] + lnb_ref[...]
        yb = ln * _sigmoid(ln)
        ys[rows, OUT_B] = _rms(yb, gg_ref[:, OUT_B]).astype(BF16)
        xc = cx_ref[r0:r0 + rb, :] * ccw_ref[0:1, :]
        for k in range(1, K_C):
            xc = xc + cx_ref[r0 + k * nb:r0 + k * nb + rb, :] * ccw_ref[k:k + 1, :]
        xc = xc + ccb_ref[...]
        xc_ref[rows, :] = xc
        xcb_ref[rows, :] = xc.astype(BF16)
        if (r0 + rb) % gm == 0:
            grows = slice(r0 + rb - gm, r0 + rb)
            for half in range(2):
                cols = slice(half * HALF_C, (half + 1) * HALF_C)
                g_ref[grows, cols] = jnp.dot(xcb_ref[grows, cols], wa_ref[half],
                                             preferred_element_type=F32)
                g_ref[grows, W_C + half * HALF_C:W_C + (half + 1) * HALF_C] = jnp.dot(
                    xcb_ref[grows, cols], wx_ref[half], preferred_element_type=F32)

    def coeffs(r):
        rows = slice(r * rb, r * rb + rb)
        rg = _sigmoid(g_ref[rows, GATE_R] + ba_ref[...])
        ig = _sigmoid(g_ref[rows, GATE_I] + bx_ref[...])
        log_a = rg * neg_c_sp
        a = jnp.exp(log_a)
        v = jnp.maximum(jnp.tanh(-log_a) * (1.0 + a * a), 0.0)
        gain = jnp.where(v > 0.0, v * lax.rsqrt(v), v)
        a_ref[rows, :] = a
        b_ref[rows, :] = gain * (ig * xc_ref[rows, :])

    def gate(r):
        rows = slice(r * rb, r * rb + rb)
        cg = us[rows, COL_C_G]
        gelu = 0.5 * cg * (1.0 + jnp.tanh(0.7978845608028654 * (cg + 0.044715 * (cg * cg * cg))))
        yc = gelu * b_ref[rows, :]
        ys[rows, OUT_C] = _rms(yc, gg_ref[:, OUT_C]).astype(BF16)

    mm_units = ([functools.partial(in_proj, c) for c in range(n_in_proj) if stages != 'no_in']
                + [functools.partial(out_proj, c) for c in range(n_out_proj) if stages != 'no_out'])
    ew_units = ([(functools.partial(convs, r), 5.0) for r in range(n_rb)]
                + [(functools.partial(coeffs, r), 1.5) for r in range(n_rb)])
    ew_total = sum(cost for _, cost in ew_units) + 1.0 * n_rb
    n_mm_done, ew_done = 0, 0.0

    def emit_mm(upto):
        nonlocal n_mm_done
        if pipelined:
            target = min(len(mm_units), int(upto * len(mm_units) + 0.999))
        else:
            target = n_in_proj if upto < 1.0 else len(mm_units)
        while n_mm_done < target:
            mm_units[n_mm_done]()
            n_mm_done += 1

    emit_mm(1.0 / len(mm_units))
    for unit, cost in ew_units:
        unit()
        ew_done += cost
        emit_mm(ew_done / ew_total)

    h = h_ref[...]
    for t in range(tt):
        rows = slice(t * nb, t * nb + nb)
        h = a_ref[rows, :] * h + b_ref[rows, :]
        b_ref[rows, :] = h

    for r in range(n_rb):
        gate(r)
        ew_done += 1.0
        emit_mm(ew_done / ew_total)
    emit_mm(1.0)

    ta = pa_ref[tm:tm + ha, :]
    tb = gb_ref[tm:tm + hb, :]
    tc = cx_ref[tm:tm + hc, :]
    pa_ref[0:ha, :] = ta
    gb_ref[0:hb, :] = tb
    cx_ref[0:hc, :] = tc
    h_ref[...] = h


def _mixer_call(x, init_a, init_b, init_c, init_h, layer, w, *, nb, tm):
    m = x.shape[0]
    n_tiles = m // tm
    ha, hb, hc = (K_A - 1) * nb, (K_B - 1) * nb, (K_C - 1) * nb
    rb = max(nb, 64)
    gm = min(tm, 256)
    init = lambda shape: pl.BlockSpec((None,) + shape, lambda i: (layer, 0, 0))
    per_channel = lambda c: pl.BlockSpec((DEPTH, c), lambda i: (0, 0))
    tile_new = pl.BlockSpec((tm, D_MODEL), lambda i: (jnp.minimum(i, n_tiles - 1), 0))
    tile_old = pl.BlockSpec((tm, D_MODEL), lambda i: (jnp.clip(i - 2, 0, n_tiles - 1), 0))
    return pl.pallas_call(
        functools.partial(_mixer_kernel, nb=nb, tm=tm, rb=rb, gm=gm, n_tiles=n_tiles, layer=layer),
        grid=(n_tiles + 2 if n_tiles > 1 else 1,),
        in_specs=[
            tile_new, tile_old,
            init((ha, W_A)), init((hb, W_B)), init((hc, W_C)), init((nb, W_C)),
            per_channel(D_MODEL),
            _const_spec((D_MODEL, IN_WIDTH), layer),
            _const_spec((K_A, W_A), layer),
            _const_spec((K_B, W_B), layer),
            per_channel(W_B),
            per_channel(W_B),
            per_channel(W_B),
            _const_spec((K_C, W_C), layer),
            per_channel(W_C),
            _const_spec((2, HALF_C, HALF_C), layer),
            per_channel(W_C),
            _const_spec((2, HALF_C, HALF_C), layer),
            per_channel(W_C),
            per_channel(W_C),
            per_channel(D_MODEL),
            _const_spec((D_MODEL, D_MODEL), layer),
        ],
        out_specs=[
            tile_old,
            init((ha, W_A)), init((hb, W_B)), init((hc, W_C)), init((nb, W_C)),
        ],
        out_shape=[jax.ShapeDtypeStruct((m, D_MODEL), F32)]
        + [jax.ShapeDtypeStruct(s.shape, F32) for s in (init_a, init_b, init_c, init_h)],
        input_output_aliases={2: 1, 3: 2, 4: 3, 5: 4},
        scratch_shapes=[
            pltpu.VMEM((tm, IN_WIDTH), F32),
            pltpu.VMEM((tm, IN_WIDTH), F32),
            pltpu.VMEM((ha + tm, W_A), F32),
            pltpu.VMEM((hb + tm, W_B), F32),
            pltpu.VMEM((hc + tm, W_C), F32),
            pltpu.VMEM((tm, W_C), F32),
            pltpu.VMEM((tm, W_C), BF16),
            pltpu.VMEM((tm, 2 * W_C), F32),
            pltpu.VMEM((tm, W_C), F32),
            pltpu.VMEM((tm, W_C), F32),
            pltpu.VMEM((nb, W_C), F32),
            pltpu.VMEM((tm, D_MODEL), BF16),
            pltpu.VMEM((tm, D_MODEL), BF16),
            pltpu.VMEM((tm, D_MODEL), BF16),
        ],
        compiler_params=pltpu.CompilerParams(
            dimension_semantics=("arbitrary",), vmem_limit_bytes=VMEM_LIMIT_BYTES),
        name="mixer",
    )(x, x, init_a, init_b, init_c, init_h, *w)


def _block_diag(w):
    half = LRU_HEADS // 2
    eye = jnp.eye(half, dtype=w.dtype)
    w = w.reshape(DEPTH, 2, half, LRU_HD, LRU_HD)
    full = jnp.einsum('lshij,hg->lshigj', w, eye)
    return full.reshape(DEPTH, 2, half * LRU_HD, half * LRU_HD).astype(BF16)


def kernel(x_prompt, x_sample, state_conv_a, state_conv_b, state_conv_c, state_lru_h, norm_ffn1, w1_up, w1_down, norm_mix, w_in, conv_a_w, conv_b_w, conv_b_b, ln_b_g, ln_b_b, conv_c_w, conv_c_b, lru_wa, lru_ba, lru_wx, lru_bx, lru_lam, grp_g, w_out, norm_ffn2, w2_up, w2_down, final_norm):
    bp, seq, _ = x_prompt.shape
    bs = x_sample.shape[0]

    p = {
        'g1': norm_ffn1, 'wup1': w1_up, 'wd1': w1_down,
        'g2': norm_ffn2, 'wup2': w2_up, 'wd2': w2_down,
        'gf': final_norm[None, :],
        'mix': (norm_mix, w_in.astype(BF16), conv_a_w, conv_b_w, conv_b_b, ln_b_g, ln_b_b,
                conv_c_w, conv_c_b, _block_diag(lru_wa), lru_ba, _block_diag(lru_wx), lru_bx,
                lru_lam, grp_g, w_out.astype(BF16)),
    }

    tmaj = lambda s: s.transpose(0, 2, 1, 3)
    rows = lambda s: s.reshape(s.shape[0], -1, s.shape[-1])
    p_st = (jnp.zeros((DEPTH, (K_A - 1) * bp, W_A), F32), jnp.zeros((DEPTH, (K_B - 1) * bp, W_B), F32),
            jnp.zeros((DEPTH, (K_C - 1) * bp, W_C), F32), jnp.zeros((DEPTH, bp, W_C), F32))
    s_st = (rows(tmaj(state_conv_a)), rows(tmaj(state_conv_b)), rows(tmaj(state_conv_c)), state_lru_h)
    xp, xs = x_prompt, x_sample.reshape(bs, D_MODEL)
    for l in range(DEPTH):
        xp, xs = _ffn_call(xp, xs, l, p['g1'], p['wup1'], p['wd1'], p['gf'], tm=TM_FFN, final=False,
                           layout='from_seq' if l == 0 else 'rows', nb=bp)
        xp, *p_st = _mixer_call(xp, *p_st, l, p['mix'], nb=bp, tm=TM_PROMPT)
        xs, *s_st = _mixer_call(xs, *s_st, l, p['mix'], nb=bs, tm=bs)
        xp, xs = _ffn_call(xp, xs, l, p['g2'], p['wup2'], p['wd2'], p['gf'], tm=TM_FFN,
                           final=(l == DEPTH - 1), layout='to_seq' if l == DEPTH - 1 else 'rows', nb=bp)
    y_prompt, ys = xp, xs

    def seq_major(st, nb):
        a, b, c, h = st
        unrow = lambda s, k: tmaj(s.reshape(DEPTH, k - 1, nb, s.shape[-1]))
        return unrow(a, K_A), unrow(b, K_B), unrow(c, K_C), h

    pa, pb, pc, ph = seq_major(p_st, bp)
    sa, sb, sc, sh = seq_major(s_st, bs)
    y_sample = ys.reshape(bs, 1, D_MODEL)
    return (y_prompt, y_sample, pa, pb, pc, ph, sa, sb, sc, sh)
```

```python
import functools

import jax
import jax.numpy as jnp
from jax import lax
from jax.experimental import pallas as pl
from jax.experimental.pallas import tpu as pltpu

D_MODEL = 1024
DEPTH = 4
W_A = 256
W_B = 256
W_C = 512
K_A = 3
K_B = 31
K_C = 4
LRU_HEADS = 8
LRU_HD = W_C // LRU_HEADS
LRU_C = 8.0
D_FF = 2816
RMS_EPS = 1e-6
LN_EPS = 1e-5
IN_WIDTH = 3 * W_A + 2 * W_B + 2 * W_C


def _column_slices(widths):
    bounds = [sum(widths[:k]) for k in range(len(widths) + 1)]
    return tuple(slice(lo, hi) for lo, hi in zip(bounds[:-1], bounds[1:]))


COL_A_B, COL_A_C, COL_A_X, COL_B_V, COL_B_G, COL_C_G, COL_C_X = _column_slices(
    (W_A, W_A, W_A, W_B, W_B, W_C, W_C))
OUT_A, OUT_B, OUT_C = _column_slices((W_A, W_B, W_C))
GATE_R, GATE_I = _column_slices((W_C, W_C))
HALF_C = W_C // 2

FF_CHUNK = 256
N_FF_CHUNKS = D_FF // FF_CHUNK
MM_CHUNK = 256
TM_PROMPT = 512
TM_FFN = 1024
SUBLANES = 8
VMEM_LIMIT_BYTES = 62 * 1024 * 1024

F32 = jnp.float32
BF16 = jnp.bfloat16


def _rms(x, g):
    return x * lax.rsqrt(jnp.mean(x * x, axis=-1, keepdims=True) + RMS_EPS) * g


def _sigmoid(x):
    return 1.0 / (1.0 + jnp.exp(-x))


def _const_spec(shape, layer):
    nd = len(shape)
    return pl.BlockSpec((None,) + tuple(shape), lambda i: (layer,) + (0,) * nd,
                        pipeline_mode=pl.Buffered(1))


def _swiglu_half_step(x, g_ref, wup_ref, wd_ref, gf_ref, final):
    h = _rms(x, g_ref[...]).astype(BF16)
    acc = jnp.zeros(x.shape, F32)
    for c in range(N_FF_CHUNKS):
        lo = c * FF_CHUNK
        wg = wup_ref[:, lo:lo + FF_CHUNK].astype(BF16)
        wu = wup_ref[:, D_FF + lo:D_FF + lo + FF_CHUNK].astype(BF16)
        wd = wd_ref[lo:lo + FF_CHUNK, :].astype(BF16)
        g = jnp.dot(h, wg, preferred_element_type=F32)
        u = jnp.dot(h, wu, preferred_element_type=F32)
        a = (g * _sigmoid(g) * u).astype(BF16)
        acc = acc + jnp.dot(a, wd, preferred_element_type=F32)
    y = x + 0.5 * acc
    if final:
        y = _rms(y, gf_ref[...])
    return y


def _ffn_kernel(x_ref, xs_ref, g_ref, wup_ref, wd_ref, gf_ref, o_ref, os_ref, *, final, layout, nb, layer):
    g_ref = g_ref.at[pl.ds(layer, 1)]
    if layout == 'from_seq':
        tq = x_ref.shape[1]
        x = x_ref[...].reshape(nb * tq, D_MODEL)
    else:
        x = x_ref[...]
    y = _swiglu_half_step(x, g_ref, wup_ref, wd_ref, gf_ref, final)
    if layout == 'from_seq':
        o_ref[...] = jnp.swapaxes(y.reshape(nb, tq, D_MODEL), 0, 1).reshape(tq * nb, D_MODEL)
    elif layout == 'to_seq':
        o_ref[...] = jnp.swapaxes(y.reshape(y.shape[0] // nb, nb, D_MODEL), 0, 1)
    else:
        o_ref[...] = y

    @pl.when(pl.program_id(0) == pl.num_programs(0) - 1)
    def _():
        os_ref[...] = _swiglu_half_step(xs_ref[...], g_ref, wup_ref, wd_ref, gf_ref, final)


def _ffn_call(x, xs, layer, norm_g, wup, wd, final_g, *, tm, final, layout='rows', nb=1):
    m = x.shape[0] * x.shape[1] if layout == 'from_seq' else x.shape[0]
    ms = xs.shape[0]
    rows_spec = pl.BlockSpec((tm, D_MODEL), lambda i: (i, 0))
    seq_spec = pl.BlockSpec((nb, tm // nb, D_MODEL), lambda i: (0, i, 0))
    sample_spec = pl.BlockSpec((ms, D_MODEL), lambda i: (0, 0))
    return pl.pallas_call(
        functools.partial(_ffn_kernel, final=final, layout=layout, nb=nb, layer=layer),
        grid=(m // tm,),
        in_specs=[
            seq_spec if layout == 'from_seq' else rows_spec,
            sample_spec,
            pl.BlockSpec((DEPTH, D_MODEL), lambda i: (0, 0)),
            _const_spec((D_MODEL, 2 * D_FF), layer),
            _const_spec((D_FF, D_MODEL), layer),
            pl.BlockSpec((1, D_MODEL), lambda i: (0, 0)),
        ],
        out_specs=[seq_spec if layout == 'to_seq' else rows_spec, sample_spec],
        out_shape=[
            jax.ShapeDtypeStruct((nb, m // nb, D_MODEL) if layout == 'to_seq' else (m, D_MODEL), F32),
            jax.ShapeDtypeStruct((ms, D_MODEL), F32),
        ],
        compiler_params=pltpu.CompilerParams(
            dimension_semantics=("arbitrary",), vmem_limit_bytes=VMEM_LIMIT_BYTES),
        name="ffn",
    )(x, xs, norm_g, wup, wd, final_g)


def _mixer_kernel(x_ref, ia_ref, ib_ref, ic_ref, ih_ref,
                  gmix_ref, win_ref, caw_ref, cbw_ref, cbb_ref, lng_ref, lnb_ref,
                  ccw_ref, ccb_ref, wa_ref, ba_ref, wx_ref, bx_ref, lam_ref, gg_ref, wout_ref,
                  o_ref, oa_ref, ob_ref, oc_ref, oh_ref,
                  u_ref, pa_ref, gb_ref, cx_ref, xc_ref, xcb_ref, g_ref, a_ref, b_ref, h_ref, y_ref, hx_ref,
                  *, nb, tm, rb, gm, layer):
    i = pl.program_id(0)
    gmix_ref, cbb_ref, lng_ref, lnb_ref, ccb_ref, ba_ref, bx_ref, lam_ref, gg_ref = (
        r.at[pl.ds(layer, 1)] for r in (gmix_ref, cbb_ref, lng_ref, lnb_ref, ccb_ref, ba_ref, bx_ref,
                                        lam_ref, gg_ref))
    ha, hb, hc = (K_A - 1) * nb, (K_B - 1) * nb, (K_C - 1) * nb
    n_rb = tm // rb
    tt = tm // nb

    @pl.when(i == 0)
    def _():
        pa_ref[0:ha, :] = ia_ref[...]
        gb_ref[0:hb, :] = ib_ref[...]
        cx_ref[0:hc, :] = ic_ref[...]
        h_ref[...] = ih_ref[...]

    hx_ref[...] = _rms(x_ref[...], gmix_ref[...]).astype(BF16)
    for c in range(IN_WIDTH // MM_CHUNK):
        cols = slice(c * MM_CHUNK, (c + 1) * MM_CHUNK)
        u_ref[:, cols] = jnp.dot(hx_ref[...], win_ref[:, cols], preferred_element_type=F32)

    sp = lam_ref[...]
    sp = jnp.maximum(-sp, 0.0) + jnp.log1p(jnp.exp(-jnp.abs(sp)))
    neg_c_sp = (-LRU_C) * sp

    for r in range(n_rb):
        r0 = r * rb
        rows = slice(r0, r0 + rb)
        pa_ref[ha + r0:ha + r0 + rb, :] = u_ref[rows, COL_A_C] * u_ref[rows, COL_A_X]
        gb_ref[hb + r0:hb + r0 + rb, :] = u_ref[rows, COL_B_V] * _sigmoid(u_ref[rows, COL_B_G])
        cx_ref[hc + r0:hc + r0 + rb, :] = u_ref[rows, COL_C_X]
        ca = pa_ref[r0:r0 + rb, :] * caw_ref[0:1, :]
        for k in range(1, K_A):
            ca = ca + pa_ref[r0 + k * nb:r0 + k * nb + rb, :] * caw_ref[k:k + 1, :]
        ya = u_ref[rows, COL_A_B] * ca
        y_ref[rows, OUT_A] = _rms(ya, gg_ref[:, OUT_A]).astype(BF16)
        cb = gb_ref[r0:r0 + rb, :] * cbw_ref[0:1, :]
        for k in range(1, K_B):
            cb = cb + gb_ref[r0 + k * nb:r0 + k * nb + rb, :] * cbw_ref[k:k + 1, :]
        cb = cb + cbb_ref[...]
        mu = jnp.mean(cb, axis=-1, keepdims=True)
        cc = cb - mu
        var = jnp.mean(cc * cc, axis=-1, keepdims=True)
        ln = cc * lax.rsqrt(var + LN_EPS) * lng_ref[...] + lnb_ref[...]
        yb = ln * _sigmoid(ln)
        y_ref[rows, OUT_B] = _rms(yb, gg_ref[:, OUT_B]).astype(BF16)
        xc = cx_ref[r0:r0 + rb, :] * ccw_ref[0:1, :]
        for k in range(1, K_C):
            xc = xc + cx_ref[r0 + k * nb:r0 + k * nb + rb, :] * ccw_ref[k:k + 1, :]
        xc = xc + ccb_ref[...]
        xc_ref[rows, :] = xc
        xcb_ref[rows, :] = xc.astype(BF16)
        if (r0 + rb) % gm == 0:
            grows = slice(r0 + rb - gm, r0 + rb)
            for half in range(2):
                cols = slice(half * HALF_C, (half + 1) * HALF_C)
                g_ref[grows, cols] = jnp.dot(xcb_ref[grows, cols], wa_ref[half],
                                             preferred_element_type=F32)
                g_ref[grows, W_C + half * HALF_C:W_C + (half + 1) * HALF_C] = jnp.dot(
                    xcb_ref[grows, cols], wx_ref[half], preferred_element_type=F32)

    for r in range(n_rb):
        rows = slice(r * rb, r * rb + rb)
        rg = _sigmoid(g_ref[rows, GATE_R] + ba_ref[...])
        ig = _sigmoid(g_ref[rows, GATE_I] + bx_ref[...])
        log_a = rg * neg_c_sp
        a = jnp.exp(log_a)
        v = jnp.maximum(jnp.tanh(-log_a) * (1.0 + a * a), 0.0)
        gain = jnp.where(v > 0.0, v * lax.rsqrt(v), v)
        a_ref[rows, :] = a
        b_ref[rows, :] = gain * (ig * xc_ref[rows, :])

    h = h_ref[...]
    for t in range(tt):
        rows = slice(t * nb, t * nb + nb)
        h = a_ref[rows, :] * h + b_ref[rows, :]
        b_ref[rows, :] = h

    for r in range(n_rb):
        rows = slice(r * rb, r * rb + rb)
        cg = u_ref[rows, COL_C_G]
        gelu = 0.5 * cg * (1.0 + jnp.tanh(0.7978845608028654 * (cg + 0.044715 * (cg * cg * cg))))
        yc = gelu * b_ref[rows, :]
        y_ref[rows, OUT_C] = _rms(yc, gg_ref[:, OUT_C]).astype(BF16)

    for c in range(D_MODEL // MM_CHUNK):
        cols = slice(c * MM_CHUNK, (c + 1) * MM_CHUNK)
        o_ref[:, cols] = x_ref[:, cols] + jnp.dot(y_ref[...], wout_ref[:, cols],
                                                  preferred_element_type=F32)

    ta = pa_ref[tm:tm + ha, :]
    tb = gb_ref[tm:tm + hb, :]
    tc = cx_ref[tm:tm + hc, :]
    pa_ref[0:ha, :] = ta
    gb_ref[0:hb, :] = tb
    cx_ref[0:hc, :] = tc
    h_ref[...] = h

    @pl.when(i == pl.num_programs(0) - 1)
    def _():
        oa_ref[...] = ta
        ob_ref[...] = tb
        oc_ref[...] = tc
        oh_ref[...] = h


def _mixer_call(x, init_a, init_b, init_c, init_h, layer, w, *, nb, tm):
    m = x.shape[0]
    ha, hb, hc = (K_A - 1) * nb, (K_B - 1) * nb, (K_C - 1) * nb
    rb = max(nb, 64)
    gm = min(tm, 256)
    init = lambda shape: pl.BlockSpec((None,) + shape, lambda i: (layer, 0, 0))
    per_channel = lambda c: pl.BlockSpec((DEPTH, c), lambda i: (0, 0))
    tile = pl.BlockSpec((tm, D_MODEL), lambda i: (i, 0))
    return pl.pallas_call(
        functools.partial(_mixer_kernel, nb=nb, tm=tm, rb=rb, gm=gm, layer=layer),
        grid=(m // tm,),
        in_specs=[
            tile,
            init((ha, W_A)), init((hb, W_B)), init((hc, W_C)), init((nb, W_C)),
            per_channel(D_MODEL),
            _const_spec((D_MODEL, IN_WIDTH), layer),
            _const_spec((K_A, W_A), layer),
            _const_spec((K_B, W_B), layer),
            per_channel(W_B),
            per_channel(W_B),
            per_channel(W_B),
            _const_spec((K_C, W_C), layer),
            per_channel(W_C),
            _const_spec((2, HALF_C, HALF_C), layer),
            per_channel(W_C),
            _const_spec((2, HALF_C, HALF_C), layer),
            per_channel(W_C),
            per_channel(W_C),
            per_channel(D_MODEL),
            _const_spec((D_MODEL, D_MODEL), layer),
        ],
        out_specs=[
            tile,
            init((ha, W_A)), init((hb, W_B)), init((hc, W_C)), init((nb, W_C)),
        ],
        out_shape=[jax.ShapeDtypeStruct((m, D_MODEL), F32)]
        + [jax.ShapeDtypeStruct(s.shape, F32) for s in (init_a, init_b, init_c, init_h)],
        input_output_aliases={1: 1, 2: 2, 3: 3, 4: 4},
        scratch_shapes=[
            pltpu.VMEM((tm, IN_WIDTH), F32),
            pltpu.VMEM((ha + tm, W_A), F32),
            pltpu.VMEM((hb + tm, W_B), F32),
            pltpu.VMEM((hc + tm, W_C), F32),
            pltpu.VMEM((tm, W_C), F32),
            pltpu.VMEM((tm, W_C), BF16),
            pltpu.VMEM((tm, 2 * W_C), F32),
            pltpu.VMEM((tm, W_C), F32),
            pltpu.VMEM((tm, W_C), F32),
            pltpu.VMEM((nb, W_C), F32),
            pltpu.VMEM((tm, D_MODEL), BF16),
            pltpu.VMEM((tm, D_MODEL), BF16),
        ],
        compiler_params=pltpu.CompilerParams(
            dimension_semantics=("arbitrary",), vmem_limit_bytes=VMEM_LIMIT_BYTES),
        name="mixer",
    )(x, init_a, init_b, init_c, init_h, *w)


def _block_diag(w):
    half = LRU_HEADS // 2
    eye = jnp.eye(half, dtype=w.dtype)
    w = w.reshape(DEPTH, 2, half, LRU_HD, LRU_HD)
    full = jnp.einsum('lshij,hg->lshigj', w, eye)
    return full.reshape(DEPTH, 2, half * LRU_HD, half * LRU_HD).astype(BF16)


def kernel(x_prompt, x_sample, state_conv_a, state_conv_b, state_conv_c, state_lru_h, norm_ffn1, w1_up, w1_down, norm_mix, w_in, conv_a_w, conv_b_w, conv_b_b, ln_b_g, ln_b_b, conv_c_w, conv_c_b, lru_wa, lru_ba, lru_wx, lru_bx, lru_lam, grp_g, w_out, norm_ffn2, w2_up, w2_down, final_norm):
    bp, seq, _ = x_prompt.shape
    bs = x_sample.shape[0]

    p = {
        'g1': norm_ffn1, 'wup1': w1_up, 'wd1': w1_down,
        'g2': norm_ffn2, 'wup2': w2_up, 'wd2': w2_down,
        'gf': final_norm[None, :],
        'mix': (norm_mix, w_in.astype(BF16), conv_a_w, conv_b_w, conv_b_b, ln_b_g, ln_b_b,
                conv_c_w, conv_c_b, _block_diag(lru_wa), lru_ba, _block_diag(lru_wx), lru_bx,
                lru_lam, grp_g, w_out.astype(BF16)),
    }

    tmaj = lambda s: s.transpose(0, 2, 1, 3)
    rows = lambda s: s.reshape(s.shape[0], -1, s.shape[-1])
    p_st = (jnp.zeros((DEPTH, (K_A - 1) * bp, W_A), F32), jnp.zeros((DEPTH, (K_B - 1) * bp, W_B), F32),
            jnp.zeros((DEPTH, (K_C - 1) * bp, W_C), F32), jnp.zeros((DEPTH, bp, W_C), F32))
    s_st = (rows(tmaj(state_conv_a)), rows(tmaj(state_conv_b)), rows(tmaj(state_conv_c)), state_lru_h)
    xp, xs = x_prompt, x_sample.reshape(bs, D_MODEL)
    for l in range(DEPTH):
        xp, xs = _ffn_call(xp, xs, l, p['g1'], p['wup1'], p['wd1'], p['gf'], tm=TM_FFN, final=False,
                           layout='from_seq' if l == 0 else 'rows', nb=bp)
        xp, *p_st = _mixer_call(xp, *p_st, l, p['mix'], nb=bp, tm=TM_PROMPT)
        xs, *s_st = _mixer_call(xs, *s_st, l, p['mix'], nb=bs, tm=bs)
        xp, xs = _ffn_call(xp, xs, l, p['g2'], p['wup2'], p['wd2'], p['gf'], tm=TM_FFN,
                           final=(l == DEPTH - 1), layout='to_seq' if l == DEPTH - 1 else 'rows', nb=bp)
    y_prompt, ys = xp, xs

    def seq_major(st, nb):
        a, b, c, h = st
        unrow = lambda s, k: tmaj(s.reshape(DEPTH, k - 1, nb, s.shape[-1]))
        return unrow(a, K_A), unrow(b, K_B), unrow(c, K_C), h

    pa, pb, pc, ph = seq_major(p_st, bp)
    sa, sb, sc, sh = seq_major(s_st, bs)
    y_sample = ys.reshape(bs, 1, D_MODEL)
    return (y_prompt, y_sample, pa, pb, pc, ph, sa, sb, sc, sh)
```

```python
import functools

import jax
import jax.numpy as jnp
from jax import lax
from jax.experimental import pallas as pl
from jax.experimental.pallas import tpu as pltpu

D_MODEL = 1024
DEPTH = 4
W_A = 256
W_B = 256
W_C = 512
K_A = 3
K_B = 31
K_C = 4
LRU_HEADS = 8
LRU_HD = W_C // LRU_HEADS
LRU_C = 8.0
D_FF = 2816
RMS_EPS = 1e-6
LN_EPS = 1e-5
IN_WIDTH = 3 * W_A + 2 * W_B + 2 * W_C


def _column_slices(widths):
    bounds = [sum(widths[:k]) for k in range(len(widths) + 1)]
    return tuple(slice(lo, hi) for lo, hi in zip(bounds[:-1], bounds[1:]))


COL_A_B, COL_A_C, COL_A_X, COL_B_V, COL_B_G, COL_C_G, COL_C_X = _column_slices(
    (W_A, W_A, W_A, W_B, W_B, W_C, W_C))
OUT_A, OUT_B, OUT_C = _column_slices((W_A, W_B, W_C))
GATE_R, GATE_I = _column_slices((W_C, W_C))
HALF_C = W_C // 2

FF_CHUNK = 256
N_FF_CHUNKS = D_FF // FF_CHUNK
MM_CHUNK = 256
TM_PROMPT = 1024
TM_FFN = 1024
SUBLANES = 8
VMEM_LIMIT_BYTES = 62 * 1024 * 1024

F32 = jnp.float32
BF16 = jnp.bfloat16


def _rms(x, g):
    return x * lax.rsqrt(jnp.mean(x * x, axis=-1, keepdims=True) + RMS_EPS) * g


def _sigmoid(x):
    return 1.0 / (1.0 + jnp.exp(-x))


def _const_spec(shape, layer):
    nd = len(shape)
    return pl.BlockSpec((None,) + tuple(shape), lambda i: (layer,) + (0,) * nd,
                        pipeline_mode=pl.Buffered(1))


def _swiglu_half_step(x, g_ref, wup_ref, wd_ref, gf_ref, final):
    h = _rms(x, g_ref[...]).astype(BF16)
    acc = jnp.zeros(x.shape, F32)
    for c in range(N_FF_CHUNKS):
        lo = c * FF_CHUNK
        wg = wup_ref[:, lo:lo + FF_CHUNK].astype(BF16)
        wu = wup_ref[:, D_FF + lo:D_FF + lo + FF_CHUNK].astype(BF16)
        wd = wd_ref[lo:lo + FF_CHUNK, :].astype(BF16)
        g = jnp.dot(h, wg, preferred_element_type=F32)
        u = jnp.dot(h, wu, preferred_element_type=F32)
        a = (g * _sigmoid(g) * u).astype(BF16)
        acc = acc + jnp.dot(a, wd, preferred_element_type=F32)
    y = x + 0.5 * acc
    if final:
        y = _rms(y, gf_ref[...])
    return y


def _ffn_kernel(x_ref, xs_ref, g_ref, wup_ref, wd_ref, gf_ref, o_ref, os_ref, *, final, layout, nb, layer):
    g_ref = g_ref.at[pl.ds(layer, 1)]
    if layout == 'from_seq':
        tq = x_ref.shape[1]
        x = x_ref[...].reshape(nb * tq, D_MODEL)
    else:
        x = x_ref[...]
    y = _swiglu_half_step(x, g_ref, wup_ref, wd_ref, gf_ref, final)
    if layout == 'from_seq':
        o_ref[...] = jnp.swapaxes(y.reshape(nb, tq, D_MODEL), 0, 1).reshape(tq * nb, D_MODEL)
    elif layout == 'to_seq':
        o_ref[...] = jnp.swapaxes(y.reshape(y.shape[0] // nb, nb, D_MODEL), 0, 1)
    else:
        o_ref[...] = y

    @pl.when(pl.program_id(0) == pl.num_programs(0) - 1)
    def _():
        os_ref[...] = _swiglu_half_step(xs_ref[...], g_ref, wup_ref, wd_ref, gf_ref, final)


def _ffn_call(x, xs, layer, norm_g, wup, wd, final_g, *, tm, final, layout='rows', nb=1):
    m = x.shape[0] * x.shape[1] if layout == 'from_seq' else x.shape[0]
    ms = xs.shape[0]
    rows_spec = pl.BlockSpec((tm, D_MODEL), lambda i: (i, 0))
    seq_spec = pl.BlockSpec((nb, tm // nb, D_MODEL), lambda i: (0, i, 0))
    sample_spec = pl.BlockSpec((ms, D_MODEL), lambda i: (0, 0))
    return pl.pallas_call(
        functools.partial(_ffn_kernel, final=final, layout=layout, nb=nb, layer=layer),
        grid=(m // tm,),
        in_specs=[
            seq_spec if layout == 'from_seq' else rows_spec,
            sample_spec,
            pl.BlockSpec((DEPTH, D_MODEL), lambda i: (0, 0)),
            _const_spec((D_MODEL, 2 * D_FF), layer),
            _const_spec((D_FF, D_MODEL), layer),
            pl.BlockSpec((1, D_MODEL), lambda i: (0, 0)),
        ],
        out_specs=[seq_spec if layout == 'to_seq' else rows_spec, sample_spec],
        out_shape=[
            jax.ShapeDtypeStruct((nb, m // nb, D_MODEL) if layout == 'to_seq' else (m, D_MODEL), F32),
            jax.ShapeDtypeStruct((ms, D_MODEL), F32),
        ],
        compiler_params=pltpu.CompilerParams(
            dimension_semantics=("arbitrary",), vmem_limit_bytes=VMEM_LIMIT_BYTES),
        name="ffn",
    )(x, xs, norm_g, wup, wd, final_g)


def _mixer_kernel(x_ref, ia_ref, ib_ref, ic_ref, ih_ref,
                  gmix_ref, win_ref, caw_ref, cbw_ref, cbb_ref, lng_ref, lnb_ref,
                  ccw_ref, ccb_ref, wa_ref, ba_ref, wx_ref, bx_ref, lam_ref, gg_ref, wout_ref,
                  o_ref, oa_ref, ob_ref, oc_ref, oh_ref,
                  u_ref, pa_ref, gb_ref, cx_ref, xc_ref, xcb_ref, g_ref, a_ref, b_ref, h_ref, y_ref, hx_ref,
                  *, nb, tm, rb, gm, layer):
    i = pl.program_id(0)
    gmix_ref, cbb_ref, lng_ref, lnb_ref, ccb_ref, ba_ref, bx_ref, lam_ref, gg_ref = (
        r.at[pl.ds(layer, 1)] for r in (gmix_ref, cbb_ref, lng_ref, lnb_ref, ccb_ref, ba_ref, bx_ref,
                                        lam_ref, gg_ref))
    ha, hb, hc = (K_A - 1) * nb, (K_B - 1) * nb, (K_C - 1) * nb
    n_rb = tm // rb
    tt = tm // nb

    @pl.when(i == 0)
    def _():
        pa_ref[0:ha, :] = ia_ref[...]
        gb_ref[0:hb, :] = ib_ref[...]
        cx_ref[0:hc, :] = ic_ref[...]
        h_ref[...] = ih_ref[...]

    hx_ref[...] = _rms(x_ref[...], gmix_ref[...]).astype(BF16)
    for c in range(IN_WIDTH // MM_CHUNK):
        cols = slice(c * MM_CHUNK, (c + 1) * MM_CHUNK)
        u_ref[:, cols] = jnp.dot(hx_ref[...], win_ref[:, cols], preferred_element_type=F32)

    sp = lam_ref[...]
    sp = jnp.maximum(-sp, 0.0) + jnp.log1p(jnp.exp(-jnp.abs(sp)))
    neg_c_sp = (-LRU_C) * sp

    for r in range(n_rb):
        r0 = r * rb
        rows = slice(r0, r0 + rb)
        pa_ref[ha + r0:ha + r0 + rb, :] = u_ref[rows, COL_A_C] * u_ref[rows, COL_A_X]
        gb_ref[hb + r0:hb + r0 + rb, :] = u_ref[rows, COL_B_V] * _sigmoid(u_ref[rows, COL_B_G])
        cx_ref[hc + r0:hc + r0 + rb, :] = u_ref[rows, COL_C_X]
        ca = pa_ref[r0:r0 + rb, :] * caw_ref[0:1, :]
        for k in range(1, K_A):
            ca = ca + pa_ref[r0 + k * nb:r0 + k * nb + rb, :] * caw_ref[k:k + 1, :]
        ya = u_ref[rows, COL_A_B] * ca
        y_ref[rows, OUT_A] = _rms(ya, gg_ref[:, OUT_A]).astype(BF16)
        cb = gb_ref[r0:r0 + rb, :] * cbw_ref[0:1, :]
        for k in range(1, K_B):
            cb = cb + gb_ref[r0 + k * nb:r0 + k * nb + rb, :] * cbw_ref[k:k + 1, :]
        cb = cb + cbb_ref[...]
        mu = jnp.mean(cb, axis=-1, keepdims=True)
        cc = cb - mu
        var = jnp.mean(cc * cc, axis=-1, keepdims=True)
        ln = cc * lax.rsqrt(var + LN_EPS) * lng_ref[...] + lnb_ref[...]
        yb = ln * _sigmoid(ln)
        y_ref[rows, OUT_B] = _rms(yb, gg_ref[:, OUT_B]).astype(BF16)
        xc = cx_ref[r0:r0 + rb, :] * ccw_ref[0:1, :]
        for k in range(1, K_C):
            xc = xc + cx_ref[r0 + k * nb:r0 + k * nb + rb, :] * ccw_ref[k:k + 1, :]
        xc = xc + ccb_ref[...]
        xc_ref[rows, :] = xc
        xcb_ref[rows, :] = xc.astype(BF16)
        if (r0 + rb) % gm == 0:
            grows = slice(r0 + rb - gm, r0 + rb)
            for half in range(2):
                cols = slice(half * HALF_C, (half + 1) * HALF_C)
                g_ref[grows, cols] = jnp.dot(xcb_ref[grows, cols], wa_ref[half],
                                             preferred_element_type=F32)
                g_ref[grows, W_C + half * HALF_C:W_C + (half + 1) * HALF_C] = jnp.dot(
                    xcb_ref[grows, cols], wx_ref[half], preferred_element_type=F32)

    for r in range(n_rb):
        rows = slice(r * rb, r * rb + rb)
        rg = _sigmoid(g_ref[rows, GATE_R] + ba_ref[...])
        ig = _sigmoid(g_ref[rows, GATE_I] + bx_ref[...])
        log_a = rg * neg_c_sp
        a = jnp.exp(log_a)
        v = jnp.maximum(jnp.tanh(-log_a) * (1.0 + a * a), 0.0)
        gain = jnp.where(v > 0.0, v * lax.rsqrt(v), v)
        a_ref[rows, :] = a
        b_ref[rows, :] = gain * (ig * xc_ref[rows, :])

    h = h_ref[...]
    for t in range(tt):
        rows = slice(t * nb, t * nb + nb)
        h = a_ref[rows, :] * h + b_ref[rows, :]
        b_ref[rows, :] = h

    for r in range(n_rb):
        rows = slice(r * rb, r * rb + rb)
        cg = u_ref[rows, COL_C_G]
        gelu = 0.5 * cg * (1.0 + jnp.tanh(0.7978845608028654 * (cg + 0.044715 * (cg * cg * cg))))
        yc = gelu * b_ref[rows, :]
        y_ref[rows, OUT_C] = _rms(yc, gg_ref[:, OUT_C]).astype(BF16)

    for c in range(D_MODEL // MM_CHUNK):
        cols = slice(c * MM_CHUNK, (c + 1) * MM_CHUNK)
        o_ref[:, cols] = x_ref[:, cols] + jnp.dot(y_ref[...], wout_ref[:, cols],
                                                  preferred_element_type=F32)

    ta = pa_ref[tm:tm + ha, :]
    tb = gb_ref[tm:tm + hb, :]
    tc = cx_ref[tm:tm + hc, :]
    pa_ref[0:ha, :] = ta
    gb_ref[0:hb, :] = tb
    cx_ref[0:hc, :] = tc
    h_ref[...] = h

    @pl.when(i == pl.num_programs(0) - 1)
    def _():
        oa_ref[...] = ta
        ob_ref[...] = tb
        oc_ref[...] = tc
        oh_ref[...] = h


def _mixer_call(x, init_a, init_b, init_c, init_h, layer, w, *, nb, tm):
    m = x.shape[0]
    ha, hb, hc = (K_A - 1) * nb, (K_B - 1) * nb, (K_C - 1) * nb
    rb = max(nb, 64)
    gm = min(tm, 256)
    init = lambda shape: pl.BlockSpec((None,) + shape, lambda i: (layer, 0, 0))
    per_channel = lambda c: pl.BlockSpec((DEPTH, c), lambda i: (0, 0))
    tile = pl.BlockSpec((tm, D_MODEL), lambda i: (i, 0))
    return pl.pallas_call(
        functools.partial(_mixer_kernel, nb=nb, tm=tm, rb=rb, gm=gm, layer=layer),
        grid=(m // tm,),
        in_specs=[
            tile,
            init((ha, W_A)), init((hb, W_B)), init((hc, W_C)), init((nb, W_C)),
            per_channel(D_MODEL),
            _const_spec((D_MODEL, IN_WIDTH), layer),
            _const_spec((K_A, W_A), layer),
            _const_spec((K_B, W_B), layer),
            per_channel(W_B),
            per_channel(W_B),
            per_channel(W_B),
            _const_spec((K_C, W_C), layer),
            per_channel(W_C),
            _const_spec((2, HALF_C, HALF_C), layer),
            per_channel(W_C),
            _const_spec((2, HALF_C, HALF_C), layer),
            per_channel(W_C),
            per_channel(W_C),
            per_channel(D_MODEL),
            _const_spec((D_MODEL, D_MODEL), layer),
        ],
        out_specs=[
            tile,
            init((ha, W_A)), init((hb, W_B)), init((hc, W_C)), init((nb, W_C)),
        ],
        out_shape=[jax.ShapeDtypeStruct((m, D_MODEL), F32)]
        + [jax.ShapeDtypeStruct(s.shape, F32) for s in (init_a, init_b, init_c, init_h)],
        input_output_aliases={1: 1, 2: 2, 3: 3, 4: 4},
        scratch_shapes=[
            pltpu.VMEM((tm, IN_WIDTH), F32),
            pltpu.VMEM((ha + tm, W_A), F32),
            pltpu.VMEM((hb + tm, W_B), F32),
            pltpu.VMEM((hc + tm, W_C), F32),
            pltpu.VMEM((tm, W_C), F32),
            pltpu.VMEM((tm, W_C), BF16),
            pltpu.VMEM((tm, 2 * W_C), F32),
            pltpu.VMEM((tm, W_C), F32),
            pltpu.VMEM((tm, W_C), F32),
            pltpu.VMEM((nb, W_C), F32),
            pltpu.VMEM((tm, D_MODEL), BF16),
            pltpu.VMEM((tm, D_MODEL), BF16),
        ],
        compiler_params=pltpu.CompilerParams(
            dimension_semantics=("arbitrary",), vmem_limit_bytes=VMEM_LIMIT_BYTES),
        name="mixer",
    )(x, init_a, init_b, init_c, init_h, *w)


def _block_diag(w):
    half = LRU_HEADS // 2
    eye = jnp.eye(half, dtype=w.dtype)
    w = w.reshape(DEPTH, 2, half, LRU_HD, LRU_HD)
    full = jnp.einsum('lshij,hg->lshigj', w, eye)
    return full.reshape(DEPTH, 2, half * LRU_HD, half * LRU_HD).astype(BF16)


def kernel(x_prompt, x_sample, state_conv_a, state_conv_b, state_conv_c, state_lru_h, norm_ffn1, w1_up, w1_down, norm_mix, w_in, conv_a_w, conv_b_w, conv_b_b, ln_b_g, ln_b_b, conv_c_w, conv_c_b, lru_wa, lru_ba, lru_wx, lru_bx, lru_lam, grp_g, w_out, norm_ffn2, w2_up, w2_down, final_norm):
    bp, seq, _ = x_prompt.shape
    bs = x_sample.shape[0]

    p = {
        'g1': norm_ffn1, 'wup1': w1_up, 'wd1': w1_down,
        'g2': norm_ffn2, 'wup2': w2_up, 'wd2': w2_down,
        'gf': final_norm[None, :],
        'mix': (norm_mix, w_in.astype(BF16), conv_a_w, conv_b_w, conv_b_b, ln_b_g, ln_b_b,
                conv_c_w, conv_c_b, _block_diag(lru_wa), lru_ba, _block_diag(lru_wx), lru_bx,
                lru_lam, grp_g, w_out.astype(BF16)),
    }

    tmaj = lambda s: s.transpose(0, 2, 1, 3)
    rows = lambda s: s.reshape(s.shape[0], -1, s.shape[-1])
    p_st = (jnp.zeros((DEPTH, (K_A - 1) * bp, W_A), F32), jnp.zeros((DEPTH, (K_B - 1) * bp, W_B), F32),
            jnp.zeros((DEPTH, (K_C - 1) * bp, W_C), F32), jnp.zeros((DEPTH, bp, W_C), F32))
    s_st = (rows(tmaj(state_conv_a)), rows(tmaj(state_conv_b)), rows(tmaj(state_conv_c)), state_lru_h)
    xp, xs = x_prompt, x_sample.reshape(bs, D_MODEL)
    for l in range(DEPTH):
        xp, xs = _ffn_call(xp, xs, l, p['g1'], p['wup1'], p['wd1'], p['gf'], tm=TM_FFN, final=False,
                           layout='from_seq' if l == 0 else 'rows', nb=bp)
        xp, *p_st = _mixer_call(xp, *p_st, l, p['mix'], nb=bp, tm=TM_PROMPT)
        xs, *s_st = _mixer_call(xs, *s_st, l, p['mix'], nb=bs, tm=bs)
        xp, xs = _ffn_call(xp, xs, l, p['g2'], p['wup2'], p['wd2'], p['gf'], tm=TM_FFN,
                           final=(l == DEPTH - 1), layout='to_seq' if l == DEPTH - 1 else 'rows', nb=bp)
    y_prompt, ys = xp, xs

    def seq_major(st, nb):
        a, b, c, h = st
        unrow = lambda s, k: tmaj(s.reshape(DEPTH, k - 1, nb, s.shape[-1]))
        return unrow(a, K_A), unrow(b, K_B), unrow(c, K_C), h

    pa, pb, pc, ph = seq_major(p_st, bp)
    sa, sb, sc, sh = seq_major(s_st, bs)
    y_sample = ys.reshape(bs, 1, D_MODEL)
    return (y_prompt, y_sample, pa, pb, pc, ph, sa, sb, sc, sh)
```

```python
import functools

import jax
import jax.numpy as jnp
from jax import lax
from jax.experimental import pallas as pl
from jax.experimental.pallas import tpu as pltpu

D_MODEL = 1024
DEPTH = 4
W_A = 256
W_B = 256
W_C = 512
K_A = 3
K_B = 31
K_C = 4
LRU_HEADS = 8
LRU_HD = W_C // LRU_HEADS
LRU_C = 8.0
D_FF = 2816
RMS_EPS = 1e-6
LN_EPS = 1e-5
IN_WIDTH = 3 * W_A + 2 * W_B + 2 * W_C


def _column_slices(widths):
    bounds = [sum(widths[:k]) for k in range(len(widths) + 1)]
    return tuple(slice(lo, hi) for lo, hi in zip(bounds[:-1], bounds[1:]))


COL_A_B, COL_A_C, COL_A_X, COL_B_V, COL_B_G, COL_C_G, COL_C_X = _column_slices(
    (W_A, W_A, W_A, W_B, W_B, W_C, W_C))
OUT_A, OUT_B, OUT_C = _column_slices((W_A, W_B, W_C))
GATE_R, GATE_I = _column_slices((W_C, W_C))
HALF_C = W_C // 2

FF_CHUNK = 256
N_FF_CHUNKS = D_FF // FF_CHUNK
MM_CHUNK = 256
TM_PROMPT = 1024
TM_FFN = 1024
SUBLANES = 8
VMEM_LIMIT_BYTES = 62 * 1024 * 1024

F32 = jnp.float32
BF16 = jnp.bfloat16


def _rms(x, g):
    return x * lax.rsqrt(jnp.mean(x * x, axis=-1, keepdims=True) + RMS_EPS) * g


def _sigmoid(x):
    return 1.0 / (1.0 + jnp.exp(-x))


def _const_spec(shape, layer):
    nd = len(shape)
    return pl.BlockSpec((None,) + tuple(shape), lambda i: (layer,) + (0,) * nd,
                        pipeline_mode=pl.Buffered(1))


def _swiglu_half_step(x, g_ref, wup_ref, wd_ref, gf_ref, final):
    h = _rms(x, g_ref[...]).astype(BF16)
    acc = jnp.zeros(x.shape, F32)
    for c in range(N_FF_CHUNKS):
        lo = c * FF_CHUNK
        wg = wup_ref[:, lo:lo + FF_CHUNK].astype(BF16)
        wu = wup_ref[:, D_FF + lo:D_FF + lo + FF_CHUNK].astype(BF16)
        wd = wd_ref[lo:lo + FF_CHUNK, :].astype(BF16)
        g = jnp.dot(h, wg, preferred_element_type=F32)
        u = jnp.dot(h, wu, preferred_element_type=F32)
        a = (g * _sigmoid(g) * u).astype(BF16)
        acc = acc + jnp.dot(a, wd, preferred_element_type=F32)
    y = x + 0.5 * acc
    if final:
        y = _rms(y, gf_ref[...])
    return y


def _ffn_kernel(x_ref, xs_ref, g_ref, wup_ref, wd_ref, gf_ref, o_ref, os_ref, *, final, layout, nb, layer):
    g_ref = g_ref.at[pl.ds(layer, 1)]
    if layout == 'from_seq':
        tq = x_ref.shape[1]
        x = x_ref[...].reshape(nb * tq, D_MODEL)
    else:
        x = x_ref[...]
    y = _swiglu_half_step(x, g_ref, wup_ref, wd_ref, gf_ref, final)
    if layout == 'from_seq':
        o_ref[...] = jnp.swapaxes(y.reshape(nb, tq, D_MODEL), 0, 1).reshape(tq * nb, D_MODEL)
    elif layout == 'to_seq':
        o_ref[...] = jnp.swapaxes(y.reshape(y.shape[0] // nb, nb, D_MODEL), 0, 1)
    else:
        o_ref[...] = y

    @pl.when(pl.program_id(0) == pl.num_programs(0) - 1)
    def _():
        os_ref[...] = _swiglu_half_step(xs_ref[...], g_ref, wup_ref, wd_ref, gf_ref, final)


def _ffn_call(x, xs, layer, norm_g, wup, wd, final_g, *, tm, final, layout='rows', nb=1):
    m = x.shape[0] * x.shape[1] if layout == 'from_seq' else x.shape[0]
    ms = xs.shape[0]
    rows_spec = pl.BlockSpec((tm, D_MODEL), lambda i: (i, 0))
    seq_spec = pl.BlockSpec((nb, tm // nb, D_MODEL), lambda i: (0, i, 0))
    sample_spec = pl.BlockSpec((ms, D_MODEL), lambda i: (0, 0))
    return pl.pallas_call(
        functools.partial(_ffn_kernel, final=final, layout=layout, nb=nb, layer=layer),
        grid=(m // tm,),
        in_specs=[
            seq_spec if layout == 'from_seq' else rows_spec,
            sample_spec,
            pl.BlockSpec((DEPTH, D_MODEL), lambda i: (0, 0)),
            _const_spec((D_MODEL, 2 * D_FF), layer),
            _const_spec((D_FF, D_MODEL), layer),
            pl.BlockSpec((1, D_MODEL), lambda i: (0, 0)),
        ],
        out_specs=[seq_spec if layout == 'to_seq' else rows_spec, sample_spec],
        out_shape=[
            jax.ShapeDtypeStruct((nb, m // nb, D_MODEL) if layout == 'to_seq' else (m, D_MODEL), F32),
            jax.ShapeDtypeStruct((ms, D_MODEL), F32),
        ],
        compiler_params=pltpu.CompilerParams(
            dimension_semantics=("arbitrary",), vmem_limit_bytes=VMEM_LIMIT_BYTES),
        name="ffn",
    )(x, xs, norm_g, wup, wd, final_g)


def _mixer_kernel(x_ref, ia_ref, ib_ref, ic_ref, ih_ref,
                  gmix_ref, win_ref, caw_ref, cbw_ref, cbb_ref, lng_ref, lnb_ref,
                  ccw_ref, ccb_ref, wa_ref, ba_ref, wx_ref, bx_ref, lam_ref, gg_ref, wout_ref,
                  o_ref, oa_ref, ob_ref, oc_ref, oh_ref,
                  u_ref, pa_ref, gb_ref, cx_ref, xc_ref, xcb_ref, g_ref, a_ref, b_ref, h_ref, y_ref, hx_ref,
                  *, nb, tm, rb, gm, layer):
    i = pl.program_id(0)
    gmix_ref, cbb_ref, lng_ref, lnb_ref, ccb_ref, ba_ref, bx_ref, lam_ref, gg_ref = (
        r.at[pl.ds(layer, 1)] for r in (gmix_ref, cbb_ref, lng_ref, lnb_ref, ccb_ref, ba_ref, bx_ref,
                                        lam_ref, gg_ref))
    ha, hb, hc = (K_A - 1) * nb, (K_B - 1) * nb, (K_C - 1) * nb
    n_rb = tm // rb
    tt = tm // nb

    @pl.when(i == 0)
    def _():
        pa_ref[0:ha, :] = ia_ref[...]
        gb_ref[0:hb, :] = ib_ref[...]
        cx_ref[0:hc, :] = ic_ref[...]
        h_ref[...] = ih_ref[...]

    hx_ref[...] = _rms(x_ref[...], gmix_ref[...]).astype(BF16)
    for c in range(IN_WIDTH // MM_CHUNK):
        cols = slice(c * MM_CHUNK, (c + 1) * MM_CHUNK)
        u_ref[:, cols] = jnp.dot(hx_ref[...], win_ref[:, cols], preferred_element_type=F32)

    sp = lam_ref[...]
    sp = jnp.maximum(-sp, 0.0) + jnp.log1p(jnp.exp(-jnp.abs(sp)))
    neg_c_sp = (-LRU_C) * sp

    for r in range(n_rb):
        r0 = r * rb
        rows = slice(r0, r0 + rb)
        pa_ref[ha + r0:ha + r0 + rb, :] = u_ref[rows, COL_A_C] * u_ref[rows, COL_A_X]
        gb_ref[hb + r0:hb + r0 + rb, :] = u_ref[rows, COL_B_V] * _sigmoid(u_ref[rows, COL_B_G])
        cx_ref[hc + r0:hc + r0 + rb, :] = u_ref[rows, COL_C_X]
        ca = pa_ref[r0:r0 + rb, :] * caw_ref[0:1, :]
        for k in range(1, K_A):
            ca = ca + pa_ref[r0 + k * nb:r0 + k * nb + rb, :] * caw_ref[k:k + 1, :]
        ya = u_ref[rows, COL_A_B] * ca
        y_ref[rows, OUT_A] = _rms(ya, gg_ref[:, OUT_A]).astype(BF16)
        cb = gb_ref[r0:r0 + rb, :] * cbw_ref[0:1, :]
        for k in range(1, K_B):
            cb = cb + gb_ref[r0 + k * nb:r0 + k * nb + rb, :] * cbw_ref[k:k + 1, :]
        cb = cb + cbb_ref[...]
        mu = jnp.mean(cb, axis=-1, keepdims=True)
        cc = cb - mu
        var = jnp.mean(cc * cc, axis=-1, keepdims=True)
        ln = cc * lax.rsqrt(var + LN_EPS) * lng_ref[...] + lnb_ref[...]
        yb = ln * _sigmoid(ln)
        y_ref[rows, OUT_B] = _rms(yb, gg_ref[:, OUT_B]).astype(BF16)
        xc = cx_ref[r0:r0 + rb, :] * ccw_ref[0:1, :]
        for k in range(1, K_C):
            xc = xc + cx_ref[r0 + k * nb:r0 + k * nb + rb, :] * ccw_ref[k:k + 1, :]
        xc = xc + ccb_ref[...]
        xc_ref[rows, :] = xc
        xcb_ref[rows, :] = xc.astype(BF16)
        if (r0 + rb) % gm == 0:
            grows = slice(r0 + rb - gm, r0 + rb)
            for half in range(2):
                cols = slice(half * HALF_C, (half + 1) * HALF_C)
                g_ref[grows, cols] = jnp.dot(xcb_ref[grows, cols], wa_ref[half],
                                             preferred_element_type=F32)
                g_ref[grows, W_C + half * HALF_C:W_C + (half + 1) * HALF_C] = jnp.dot(
                    xcb_ref[grows, cols], wx_ref[half], preferred_element_type=F32)

    for r in range(n_rb):
        rows = slice(r * rb, r * rb + rb)
        rg = _sigmoid(g_ref[rows, GATE_R] + ba_ref[...])
        ig = _sigmoid(g_ref[rows, GATE_I] + bx_ref[...])
        log_a = rg * neg_c_sp
        a = jnp.exp(log_a)
        v = jnp.maximum(jnp.tanh(-log_a) * (1.0 + a * a), 0.0)
        gain = jnp.where(v > 0.0, v * lax.rsqrt(v), v)
        a_ref[rows, :] = a
        b_ref[rows, :] = gain * (ig * xc_ref[rows, :])

    h = h_ref[...]
    for t in range(tt):
        rows = slice(t * nb, t * nb + nb)
        h = a_ref[rows, :] * h + b_ref[rows, :]
        b_ref[rows, :] = h

    for r in range(n_rb):
        rows = slice(r * rb, r * rb + rb)
        cg = u_ref[rows, COL_C_G]
        gelu = 0.5 * cg * (1.0 + jnp.tanh(0.7978845608028654 * (cg + 0.044715 * (cg * cg * cg))))
        yc = gelu * b_ref[rows, :]
        y_ref[rows, OUT_C] = _rms(yc, gg_ref[:, OUT_C]).astype(BF16)

    for c in range(D_MODEL // MM_CHUNK):
        cols = slice(c * MM_CHUNK, (c + 1) * MM_CHUNK)
        o_ref[:, cols] = x_ref[:, cols] + jnp.dot(y_ref[...], wout_ref[:, cols],
                                                  preferred_element_type=F32)

    ta = pa_ref[tm:tm + ha, :]
    tb = gb_ref[tm:tm + hb, :]
    tc = cx_ref[tm:tm + hc, :]
    pa_ref[0:ha, :] = ta
    gb_ref[0:hb, :] = tb
    cx_ref[0:hc, :] = tc
    h_ref[...] = h

    @pl.when(i == pl.num_programs(0) - 1)
    def _():
        oa_ref[...] = ta
        ob_ref[...] = tb
        oc_ref[...] = tc
        oh_ref[...] = h


def _mixer_call(x, init_a, init_b, init_c, init_h, layer, w, *, nb, tm):
    m = x.shape[0]
    ha, hb, hc = (K_A - 1) * nb, (K_B - 1) * nb, (K_C - 1) * nb
    rb = max(nb, 128)
    gm = min(tm, 256)
    init = lambda shape: pl.BlockSpec((None,) + shape, lambda i: (layer, 0, 0))
    per_channel = lambda c: pl.BlockSpec((DEPTH, c), lambda i: (0, 0))
    tile = pl.BlockSpec((tm, D_MODEL), lambda i: (i, 0))
    return pl.pallas_call(
        functools.partial(_mixer_kernel, nb=nb, tm=tm, rb=rb, gm=gm, layer=layer),
        grid=(m // tm,),
        in_specs=[
            tile,
            init((ha, W_A)), init((hb, W_B)), init((hc, W_C)), init((nb, W_C)),
            per_channel(D_MODEL),
            _const_spec((D_MODEL, IN_WIDTH), layer),
            _const_spec((K_A, W_A), layer),
            _const_spec((K_B, W_B), layer),
            per_channel(W_B),
            per_channel(W_B),
            per_channel(W_B),
            _const_spec((K_C, W_C), layer),
            per_channel(W_C),
            _const_spec((2, HALF_C, HALF_C), layer),
            per_channel(W_C),
            _const_spec((2, HALF_C, HALF_C), layer),
            per_channel(W_C),
            per_channel(W_C),
            per_channel(D_MODEL),
            _const_spec((D_MODEL, D_MODEL), layer),
        ],
        out_specs=[
            tile,
            init((ha, W_A)), init((hb, W_B)), init((hc, W_C)), init((nb, W_C)),
        ],
        out_shape=[jax.ShapeDtypeStruct((m, D_MODEL), F32)]
        + [jax.ShapeDtypeStruct(s.shape, F32) for s in (init_a, init_b, init_c, init_h)],
        input_output_aliases={1: 1, 2: 2, 3: 3, 4: 4},
        scratch_shapes=[
            pltpu.VMEM((tm, IN_WIDTH), F32),
            pltpu.VMEM((ha + tm, W_A), F32),
            pltpu.VMEM((hb + tm, W_B), F32),
            pltpu.VMEM((hc + tm, W_C), F32),
            pltpu.VMEM((tm, W_C), F32),
            pltpu.VMEM((tm, W_C), BF16),
            pltpu.VMEM((tm, 2 * W_C), F32),
            pltpu.VMEM((tm, W_C), F32),
            pltpu.VMEM((tm, W_C), F32),
            pltpu.VMEM((nb, W_C), F32),
            pltpu.VMEM((tm, D_MODEL), BF16),
            pltpu.VMEM((tm, D_MODEL), BF16),
        ],
        compiler_params=pltpu.CompilerParams(
            dimension_semantics=("arbitrary",), vmem_limit_bytes=VMEM_LIMIT_BYTES),
        name="mixer",
    )(x, init_a, init_b, init_c, init_h, *w)


def _block_diag(w):
    half = LRU_HEADS // 2
    eye = jnp.eye(half, dtype=w.dtype)
    w = w.reshape(DEPTH, 2, half, LRU_HD, LRU_HD)
    full = jnp.einsum('lshij,hg->lshigj', w, eye)
    return full.reshape(DEPTH, 2, half * LRU_HD, half * LRU_HD).astype(BF16)


def kernel(x_prompt, x_sample, state_conv_a, state_conv_b, state_conv_c, state_lru_h, norm_ffn1, w1_up, w1_down, norm_mix, w_in, conv_a_w, conv_b_w, conv_b_b, ln_b_g, ln_b_b, conv_c_w, conv_c_b, lru_wa, lru_ba, lru_wx, lru_bx, lru_lam, grp_g, w_out, norm_ffn2, w2_up, w2_down, final_norm):
    bp, seq, _ = x_prompt.shape
    bs = x_sample.shape[0]

    p = {
        'g1': norm_ffn1, 'wup1': w1_up, 'wd1': w1_down,
        'g2': norm_ffn2, 'wup2': w2_up, 'wd2': w2_down,
        'gf': final_norm[None, :],
        'mix': (norm_mix, w_in.astype(BF16), conv_a_w, conv_b_w, conv_b_b, ln_b_g, ln_b_b,
                conv_c_w, conv_c_b, _block_diag(lru_wa), lru_ba, _block_diag(lru_wx), lru_bx,
                lru_lam, grp_g, w_out.astype(BF16)),
    }

    tmaj = lambda s: s.transpose(0, 2, 1, 3)
    rows = lambda s: s.reshape(s.shape[0], -1, s.shape[-1])
    p_st = (jnp.zeros((DEPTH, (K_A - 1) * bp, W_A), F32), jnp.zeros((DEPTH, (K_B - 1) * bp, W_B), F32),
            jnp.zeros((DEPTH, (K_C - 1) * bp, W_C), F32), jnp.zeros((DEPTH, bp, W_C), F32))
    s_st = (rows(tmaj(state_conv_a)), rows(tmaj(state_conv_b)), rows(tmaj(state_conv_c)), state_lru_h)
    xp, xs = x_prompt, x_sample.reshape(bs, D_MODEL)
    for l in range(DEPTH):
        xp, xs = _ffn_call(xp, xs, l, p['g1'], p['wup1'], p['wd1'], p['gf'], tm=TM_FFN, final=False,
                           layout='from_seq' if l == 0 else 'rows', nb=bp)
        xp, *p_st = _mixer_call(xp, *p_st, l, p['mix'], nb=bp, tm=TM_PROMPT)
        xs, *s_st = _mixer_call(xs, *s_st, l, p['mix'], nb=bs, tm=bs)
        xp, xs = _ffn_call(xp, xs, l, p['g2'], p['wup2'], p['wd2'], p['gf'], tm=TM_FFN,
                           final=(l == DEPTH - 1), layout='to_seq' if l == DEPTH - 1 else 'rows', nb=bp)
    y_prompt, ys = xp, xs

    def seq_major(st, nb):
        a, b, c, h = st
        unrow = lambda s, k: tmaj(s.reshape(DEPTH, k - 1, nb, s.shape[-1]))
        return unrow(a, K_A), unrow(b, K_B), unrow(c, K_C), h

    pa, pb, pc, ph = seq_major(p_st, bp)
    sa, sb, sc, sh = seq_major(s_st, bs)
    y_sample = ys.reshape(bs, 1, D_MODEL)
    return (y_prompt, y_sample, pa, pb, pc, ph, sa, sb, sc, sh)
```
